```python
import math
import jax, jax.numpy as jnp
from jax import lax
import numpy as np

D_MODEL = 1024
BATCH = 16
SEQ = 256
DEPTH = 4
DEC_BATCH = 4
DEC_SEQ = 1024
PAST_LEN = 256

GRID_W = 64
DA_HEADS = 4
DA_HD = 64
DA_VD = 2 * DA_HD
DA_W = DA_HEADS * DA_VD
QBLK = 128
ROPE_THETA = 10000.0
ML_HEADS = 4
ML_HD = 128
ML_W = ML_HEADS * ML_HD
ML_CHUNK = 128
SG_GROUPS = 4
SG_CHUNK = 128
SG_W = 512
SG_GD = SG_W // SG_GROUPS
N_BRANCH = 3
BR_W = 512
D_FF = 2816
CONV_W = 3
EPS = 1e-6
NEG = -1e30
IN_SPLITS = (DA_W, 2 * DA_W, 3 * DA_W,
             3 * DA_W + ML_W, 3 * DA_W + 2 * ML_W, 3 * DA_W + 3 * ML_W, 3 * DA_W + 4 * ML_W,
             3 * DA_W + 4 * ML_W + 4 * ML_HEADS,
             3 * DA_W + 4 * ML_W + 4 * ML_HEADS + 2 * SG_W)
N_IN = 3 * DA_W + 4 * ML_W + 4 * ML_HEADS + 2 * SG_W + N_BRANCH * D_MODEL

kernel_name = 'hybrid_diffusion_prefix_trunk_step'


def rmsnorm(x, g=None):
    xf = x.astype(jnp.float32)
    y = xf * lax.rsqrt(jnp.mean(xf * xf, axis=-1, keepdims=True) + EPS)
    if g is not None:
        y = y * g.astype(jnp.float32)
    return y.astype(x.dtype)


def layernorm(x, g):
    xf = x.astype(jnp.float32)
    xc = xf - jnp.mean(xf, axis=-1, keepdims=True)
    y = xc * lax.rsqrt(jnp.mean(xc * xc, axis=-1, keepdims=True) + EPS) * g.astype(jnp.float32)
    return y.astype(x.dtype)


def dwconv3(x, w, b):
    T = x.shape[1]
    xp = jnp.pad(x, ((0, 0), (1, 1), (0, 0)))
    return xp[:, :T] * w[0] + xp[:, 1:T + 1] * w[1] + xp[:, 2:] * w[2] + b


def to_heads(a, n_heads):
    B, T, W = a.shape
    return a.reshape(B, T, n_heads, W // n_heads).transpose(0, 2, 1, 3)


def from_heads(a):
    B, H, T, Dh = a.shape
    return a.transpose(0, 2, 1, 3).reshape(B, T, H * Dh)


def axial_rope_tables(rows):
    t = jnp.arange(rows * GRID_W)
    row = (t // GRID_W).astype(jnp.float32)
    col = (t % GRID_W).astype(jnp.float32)
    nf = DA_HD // 4
    inv = ROPE_THETA ** (-jnp.arange(nf, dtype=jnp.float32) / nf)
    ang = jnp.stack([row[:, None] * inv, col[:, None] * inv], axis=1)
    return jnp.cos(ang), jnp.sin(ang)


def apply_axial_rope(x, cos, sin):
    nf = DA_HD // 4
    xs = x.reshape(x.shape[:-1] + (2, 2, 2, nf))
    x1, x2 = xs[..., 0, :], xs[..., 1, :]
    c = cos[:, None].astype(x.dtype)
    s = sin[:, None].astype(x.dtype)
    out = jnp.stack([x1 * c - x2 * s, x1 * s + x2 * c], axis=-2)
    return out.reshape(x.shape)


def diff_attention(q, k, v, lam):
    B, H, Tq, _ = q.shape
    nb = Tq // QBLK
    qb = jnp.moveaxis(q.reshape(B, H, nb, QBLK, 2 * DA_HD), 2, 0)
    k1, k2 = k[..., :DA_HD], k[..., DA_HD:]
    scale = DA_HD ** -0.5

    def block(qblk):
        q1, q2 = qblk[..., :DA_HD], qblk[..., DA_HD:]
        p1 = jax.nn.softmax((jnp.einsum('bhqd,bhkd->bhqk', q1, k1) * scale).astype(jnp.float32), axis=-1)
        p2 = jax.nn.softmax((jnp.einsum('bhqd,bhkd->bhqk', q2, k2) * scale).astype(jnp.float32), axis=-1)
        a = (p1 - lam * p2).astype(v.dtype)
        return jnp.einsum('bhqk,bhkv->bhqv', a, v)

    o = lax.map(block, qb)
    return jnp.moveaxis(o, 0, 2).reshape(B, H, Tq, DA_VD)


def mlstm_chunkwise(q, k, v, i_pre, logf, init):
    B, H, T, Dh = q.shape
    nc = T // ML_CHUNK

    def chunks(a):
        return jnp.moveaxis(a.reshape((B, H, nc, ML_CHUNK) + a.shape[3:]), 2, 0)

    causal = jnp.tril(jnp.ones((ML_CHUNK, ML_CHUNK), dtype=bool))

    def step(carry, inp):
        C, n, m = carry
        qc, kc, vc, ic, fc = inp
        b = jnp.cumsum(fc, axis=-1)
        log_w = jnp.where(causal, b[..., :, None] - b[..., None, :] + ic[..., None, :], NEG)
        inter = b + m[..., None]
        m_t = jnp.maximum(inter, jnp.max(log_w, axis=-1))
        w = jnp.exp(log_w - m_t[..., None])
        s_inter = jnp.exp(inter - m_t)
        qk = jnp.einsum('bhtd,bhsd->bhts', qc, kc) * w
        num = jnp.einsum('bhts,bhsv->bhtv', qk, vc) + s_inter[..., None] * jnp.einsum('bhtd,bhdv->bhtv', qc, C)
        den = jnp.sum(qk, axis=-1) + s_inter * jnp.einsum('bhtd,bhd->bht', qc, n)
        h = num / jnp.maximum(jnp.abs(den), jnp.exp(-m_t))[..., None]
        m_new = m_t[..., -1]
        g = jnp.exp(b[..., -1:] - b + ic - m_new[..., None])
        decay = jnp.exp(b[..., -1] + m - m_new)
        C_new = decay[..., None, None] * C + jnp.einsum('bhs,bhsd,bhsv->bhdv', g, kc, vc)
        n_new = decay[..., None] * n + jnp.einsum('bhs,bhsd->bhd', g, kc)
        return (C_new, n_new, m_new), h

    state, hs = lax.scan(step, init, (chunks(q), chunks(k), chunks(v), chunks(i_pre), chunks(logf)))
    return jnp.moveaxis(hs, 0, 2).reshape(B, H, T, Dh), state


def token_mixer(h, l, P, rope, ctx):
    B, T, _ = h.shape
    z = h @ P['w_in'][l]
    da_q, da_k, da_v, ml_q, ml_k, ml_v, ml_o, ml_g, sg_uv, gate_pre = jnp.split(z, IN_SPLITS, axis=-1)

    q = to_heads(da_q, DA_HEADS)
    k = to_heads(da_k, DA_HEADS)
    v = to_heads(da_v, DA_HEADS)
    if rope is not None:
        q = apply_axial_rope(q, rope[0], rope[1])
        k = apply_axial_rope(k, rope[0], rope[1])
    if ctx is None:
        k_all, v_all = k, v
    else:
        k_all = jnp.concatenate([ctx[0].astype(k.dtype), k], axis=2)
        v_all = jnp.concatenate([ctx[1].astype(v.dtype), v], axis=2)
    lq1, lk1, lq2, lk2 = P['da_lambda'][l].astype(jnp.float32)
    lam_init = 0.8 - 0.6 * math.exp(-0.3 * l)
    lam = jnp.exp(jnp.sum(lq1 * lk1)) - jnp.exp(jnp.sum(lq2 * lk2)) + lam_init
    o = diff_attention(q, k_all, v_all, lam)
    y_da = from_heads(rmsnorm(o, P['da_norm_g'][l]) * (1.0 - lam_init))

    qk = jax.nn.silu(dwconv3(jnp.concatenate([ml_q, ml_k], axis=-1), P['ml_conv_w'][l], P['ml_conv_b'][l]))
    mq = to_heads(qk[..., :ML_W], ML_HEADS).astype(jnp.float32)
    mk = to_heads(qk[..., ML_W:], ML_HEADS).astype(jnp.float32) * (ML_HD ** -0.5)
    mv = to_heads(ml_v, ML_HEADS).astype(jnp.float32)
    gp = (ml_g.reshape(B, T, 4, ML_HEADS).astype(jnp.float32)
          + P['ml_gate_b'][l].astype(jnp.float32)).transpose(2, 0, 3, 1)
    i_fw, i_bw = gp[0], gp[1]
    lf_fw, lf_bw = jax.nn.log_sigmoid(gp[2]), jax.nn.log_sigmoid(gp[3])
    if ctx is None:
        zero_state = (jnp.zeros((B, ML_HEADS, ML_HD, ML_HD), jnp.float32),
                      jnp.zeros((B, ML_HEADS, ML_HD), jnp.float32),
                      jnp.zeros((B, ML_HEADS), jnp.float32))
        init_fw, init_bw = zero_state, zero_state
    else:
        C0, n0, m0 = ctx[2].astype(jnp.float32), ctx[3].astype(jnp.float32), ctx[4].astype(jnp.float32)
        init_fw = (C0[:, 0], n0[:, 0], m0[:, 0])
        init_bw = (C0[:, 1], n0[:, 1], m0[:, 1])
    h_fw, st_fw = mlstm_chunkwise(mq, mk, mv, i_fw, lf_fw, init_fw)
    h_bw_r, st_bw = mlstm_chunkwise(jnp.flip(mq, 2), jnp.flip(mk, 2), jnp.flip(mv, 2),
                                    jnp.flip(i_bw, 2), jnp.flip(lf_bw, 2), init_bw)
    h_ml = rmsnorm(h_fw + jnp.flip(h_bw_r, 2), P['ml_norm_g'][l]).astype(h.dtype)
    y_ml = jax.nn.sigmoid(ml_o) * from_heads(h_ml)

    zz = jax.nn.gelu(sg_uv)
    u = zz[..., :SG_W]
    sv = layernorm(zz[..., SG_W:], P['sg_norm_g'][l])
    nc = T // SG_CHUNK
    sv = sv.reshape(B, nc, SG_CHUNK, SG_GROUPS, SG_GD)
    sv = jnp.einsum('gpq,bnqgc->bnpgc', P['sg_w'][l], sv) + P['sg_b'][l].T[:, :, None]
    y_sg = u * sv.reshape(B, T, SG_W)

    br = jnp.stack([y_da, y_ml, y_sg], axis=2)
    proj = jnp.einsum('btnc,ncd->btnd', br, P['w_branch'][l])
    gates = jax.nn.sigmoid(gate_pre).reshape(B, T, N_BRANCH, D_MODEL)
    out = jnp.sum(gates * proj, axis=2) @ P['w_out'][l]
    if ctx is None:
        ctx_out = (k, v,
                   jnp.stack([st_fw[0], st_bw[0]], axis=1),
                   jnp.stack([st_fw[1], st_bw[1]], axis=1),
                   jnp.stack([st_fw[2], st_bw[2]], axis=1))
    else:
        ctx_out = None
    return out, ctx_out


def trunk_layer(x, cond, l, P, rope, ctx):
    mod = (jax.nn.silu(cond) @ P['w_mod'][l] + P['b_mod'][l])[:, None, :]
    sh1, sc1, g1, sh2, sc2, g2 = jnp.split(mod, 6, axis=-1)
    h = rmsnorm(x) * (1.0 + sc1) + sh1
    mix, ctx_out = token_mixer(h, l, P, rope, ctx)
    x = x + g1 * mix
    h = rmsnorm(x) * (1.0 + sc2) + sh2
    u = dwconv3(h @ P['w_up'][l], P['ffn_conv_w'][l], P['ffn_conv_b'][l])
    x = x + g2 * ((jax.nn.silu(u[..., :D_FF]) * u[..., D_FF:]) @ P['w_down'][l])
    return x, ctx_out


def setup_inputs(seed: int = 0) -> dict:
    key = jax.random.key(seed)
    ks = jax.random.split(key, 32)

    def nrm(k, shape, scale):
        return scale * jax.random.normal(k, shape, jnp.float32)

    L = DEPTH
    return {
        'x_prompt': nrm(ks[0], (BATCH, SEQ, D_MODEL), 1.0),
        'x_sample': nrm(ks[1], (DEC_BATCH, DEC_SEQ, D_MODEL), 1.0),
        'c': nrm(ks[2], (DEC_BATCH, D_MODEL), 1.0),
        'cache_k': nrm(ks[3], (DEC_BATCH, L, DA_HEADS, PAST_LEN, 2 * DA_HD), 1.0),
        'cache_v': nrm(ks[4], (DEC_BATCH, L, DA_HEADS, PAST_LEN, DA_VD), 1.0),
        'state_C': nrm(ks[5], (DEC_BATCH, L, 2, ML_HEADS, ML_HD, ML_HD), 0.1),
        'state_n': nrm(ks[6], (DEC_BATCH, L, 2, ML_HEADS, ML_HD), 0.1),
        'state_m': 1.0 + nrm(ks[7], (DEC_BATCH, L, 2, ML_HEADS), 0.5),
        'c_ctx': nrm(ks[8], (D_MODEL,), 1.0),
        'w_mod': nrm(ks[9], (L, D_MODEL, 6 * D_MODEL), D_MODEL ** -0.5),
        'b_mod': nrm(ks[10], (L, 6 * D_MODEL), 0.01),
        'w_in': nrm(ks[11], (L, D_MODEL, N_IN), D_MODEL ** -0.5),
        'da_lambda': nrm(ks[12], (L, 4, DA_HD), 0.1),
        'da_norm_g': 1.0 + nrm(ks[13], (L, DA_VD), 0.01),
        'ml_conv_w': nrm(ks[14], (L, CONV_W, 2 * ML_W), CONV_W ** -0.5),
        'ml_conv_b': nrm(ks[15], (L, 2 * ML_W), 0.01),
        'ml_gate_b': jnp.concatenate([nrm(ks[16], (L, 2, ML_HEADS), 0.1),
                                      3.0 + nrm(ks[17], (L, 2, ML_HEADS), 0.1)], axis=1),
        'ml_norm_g': 1.0 + nrm(ks[18], (L, ML_HD), 0.01),
        'sg_norm_g': 1.0 + nrm(ks[19], (L, SG_W), 0.01),
        'sg_w': nrm(ks[20], (L, SG_GROUPS, SG_CHUNK, SG_CHUNK), SG_CHUNK ** -0.5),
        'sg_b': 1.0 + nrm(ks[21], (L, SG_GROUPS, SG_CHUNK), 0.01),
        'w_branch': nrm(ks[22], (L, N_BRANCH, BR_W, D_MODEL), BR_W ** -0.5),
        'w_out': nrm(ks[23], (L, D_MODEL, D_MODEL), D_MODEL ** -0.5),
        'w_up': nrm(ks[24], (L, D_MODEL, 2 * D_FF), D_MODEL ** -0.5),
        'ffn_conv_w': nrm(ks[25], (L, CONV_W, 2 * D_FF), CONV_W ** -0.5),
        'ffn_conv_b': nrm(ks[26], (L, 2 * D_FF), 0.01),
        'w_down': nrm(ks[27], (L, D_FF, D_MODEL), D_FF ** -0.5),
        'final_g': 1.0 + nrm(ks[28], (D_MODEL,), 0.01),
    }


def reference(x_prompt, x_sample, c, cache_k, cache_v, state_C, state_n, state_m, c_ctx,
              w_mod, b_mod, w_in, da_lambda, da_norm_g, ml_conv_w, ml_conv_b, ml_gate_b,
              ml_norm_g, sg_norm_g, sg_w, sg_b, w_branch, w_out, w_up, ffn_conv_w, ffn_conv_b,
              w_down, final_g):
    P = {'w_mod': w_mod, 'b_mod': b_mod, 'w_in': w_in, 'da_lambda': da_lambda, 'da_norm_g': da_norm_g,
         'ml_conv_w': ml_conv_w, 'ml_conv_b': ml_conv_b, 'ml_gate_b': ml_gate_b, 'ml_norm_g': ml_norm_g,
         'sg_norm_g': sg_norm_g, 'sg_w': sg_w, 'sg_b': sg_b, 'w_branch': w_branch, 'w_out': w_out,
         'w_up': w_up, 'ffn_conv_w': ffn_conv_w, 'ffn_conv_b': ffn_conv_b, 'w_down': w_down}

    xp = x_prompt
    ks_, vs_, Cs_, ns_, ms_ = [], [], [], [], []
    for l in range(DEPTH):
        xp, (k_l, v_l, C_l, n_l, m_l) = trunk_layer(xp, c_ctx[None, :], l, P, None, None)
        ks_.append(k_l)
        vs_.append(v_l)
        Cs_.append(C_l)
        ns_.append(n_l)
        ms_.append(m_l)
    y_prompt = rmsnorm(xp, final_g)
    new_cache_k = jnp.stack(ks_, axis=1)
    new_cache_v = jnp.stack(vs_, axis=1)
    new_state_C = jnp.stack(Cs_, axis=1).astype(x_prompt.dtype)
    new_state_n = jnp.stack(ns_, axis=1).astype(x_prompt.dtype)
    new_state_m = jnp.stack(ms_, axis=1).astype(x_prompt.dtype)

    rows = x_sample.shape[1] // GRID_W
    rope = axial_rope_tables(rows)
    xs = x_sample
    for l in range(DEPTH):
        ctx = (cache_k[:, l], cache_v[:, l], state_C[:, l], state_n[:, l], state_m[:, l])
        xs, _ = trunk_layer(xs, c, l, P, rope, ctx)
    y_sample = rmsnorm(xs, final_g)

    return (y_prompt, y_sample, new_cache_k, new_cache_v, new_state_C, new_state_n, new_state_m)
```

```python
import functools
import math

import jax
import jax.numpy as jnp
from jax import lax
from jax.experimental import pallas as pl
from jax.experimental.pallas import tpu as pltpu

F32 = jnp.float32
BF16 = jnp.bfloat16

GRID_W = 64
DA_HEADS = 4
DA_HD = 64
ML_HEADS = 4
ML_HD = 128
CHUNK = 128
SG_GROUPS = 4
BR_W = 512
N_BRANCH = 3
ROPE_THETA = 10000.0
EPS = 1e-6
NEG = -1e30

VMEM_LIMIT_BYTES = 52 * 1024 * 1024

TOKEN_TILE = 1024
COL_TILE = 512
MERGE_TILE = 512
FF_TILE = 256
Q_TILE = 256

ZJ_Q, ZJ_K, ZJ_V, ZJ_MQ, ZJ_MK, ZJ_MV, ZJ_MO, ZJ_SGU = range(8)
ZJ_GATE0 = 8
ZJ_SGV = 14
N_ZJ = 15


def _cparams(sem):
    return pltpu.CompilerParams(dimension_semantics=sem, vmem_limit_bytes=VMEM_LIMIT_BYTES)


def _dot(a, b):
    return jnp.dot(a, b, preferred_element_type=F32)


def _dot_nt(a, b):
    return lax.dot_general(a, b, (((1,), (1,)), ((), ())), preferred_element_type=F32)


def _dot_tn(a, b):
    return lax.dot_general(a, b, (((0,), (0,)), ((), ())), preferred_element_type=F32)


def _mod_kernel(cond_ref, w_ref, b_ref, o_ref):
    a = jax.nn.silu(cond_ref[...]).astype(BF16)
    o_ref[0] = _dot(a, w_ref[0].astype(BF16)) + b_ref[0]


def _modulation(cond8, w_mod, b_mod):
    L, D, N = w_mod.shape
    tn = 768
    return pl.pallas_call(
        _mod_kernel,
        out_shape=jax.ShapeDtypeStruct((L, 8, N), F32),
        grid=(L, N // tn),
        in_specs=[
            pl.BlockSpec((8, D), lambda l, j: (0, 0)),
            pl.BlockSpec((1, D, tn), lambda l, j: (l, 0, j)),
            pl.BlockSpec((1, 1, tn), lambda l, j: (l, 0, j)),
        ],
        out_specs=pl.BlockSpec((1, 8, tn), lambda l, j: (l, 0, j)),
        compiler_params=_cparams(("arbitrary", "arbitrary")),
        name="modulation",
    )(cond8, w_mod, b_mod.reshape(L, 1, N))


def _rms(x):
    return x * lax.rsqrt(jnp.mean(x * x, axis=-1, keepdims=True) + EPS)


def _seq_conv3(u, w_ref, b_ref, seq_len):
    rows = u.shape[0]
    pos = lax.broadcasted_iota(jnp.int32, (rows, 1), 0) & (seq_len - 1)
    has_prev = (pos != 0).astype(F32)
    has_next = (pos != seq_len - 1).astype(F32)
    prev = pltpu.roll(u, 1, axis=0) * has_prev
    nxt = pltpu.roll(u, rows - 1, axis=0) * has_next
    return prev * w_ref[0:1, :] + u * w_ref[1:2, :] + nxt * w_ref[2:3, :] + b_ref[...]


def _inproj_kernel(x_ref, mod_ref, wz_ref, wg_ref, cw_ref, cb_ref, cos_ref, sa_ref, sb_ref,
                   z_ref, g_ref, h_scr, *, n_ctx_tiles, ctx_len, lat_len):
    i = pl.program_id(0)
    j = pl.program_id(1)
    D = x_ref.shape[1]

    @pl.when(j == 0)
    def _():
        h = _rms(x_ref[...]) * (1.0 + mod_ref[0, :, D:2 * D]) + mod_ref[0, :, 0:D]
        hb = h.astype(BF16)
        h_scr[...] = hb
        g_ref[...] = _dot(hb, wg_ref[...])

    z = _dot(h_scr[...], wz_ref[...])
    is_lat = i >= n_ctx_tiles

    def rope(zz):
        parts = []
        for hd in range(zz.shape[1] // 128):
            xh = zz[:, hd * 128:(hd + 1) * 128]
            parts.append(xh * cos_ref[...] + pltpu.roll(xh, 112, axis=1) * sa_ref[...]
                         + pltpu.roll(xh, 16, axis=1) * sb_ref[...])
        return jnp.concatenate(parts, axis=1)

    @pl.when((j == ZJ_Q) & is_lat)
    def _():
        z_ref[...] = (rope(z) * (DA_HD ** -0.5)).astype(BF16)

    @pl.when((j == ZJ_Q) & jnp.logical_not(is_lat))
    def _():
        z_ref[...] = (z * (DA_HD ** -0.5)).astype(BF16)

    @pl.when((j == ZJ_K) & is_lat)
    def _():
        z_ref[...] = rope(z).astype(BF16)

    @pl.when(((j == ZJ_K) & jnp.logical_not(is_lat)) | (j == ZJ_V) | (j == ZJ_MV))
    def _():
        z_ref[...] = z.astype(BF16)

    @pl.when((j == ZJ_MQ) | (j == ZJ_MK))
    def _():
        seq_len = jnp.where(is_lat, lat_len, ctx_len)
        a = jax.nn.silu(_seq_conv3(z, cw_ref, cb_ref, seq_len))
        scale = jnp.where(j == ZJ_MK, ML_HD ** -0.5, 1.0).astype(F32)
        z_ref[...] = (a * scale).astype(BF16)

    @pl.when((j == ZJ_MO) | ((j >= ZJ_GATE0) & (j < ZJ_SGV)))
    def _():
        z_ref[...] = jax.nn.sigmoid(z).astype(BF16)

    @pl.when((j == ZJ_SGU) | (j == ZJ_SGV))
    def _():
        z_ref[...] = jax.nn.gelu(z, approximate=True).astype(BF16)


def _inproj(x, mod_l, wz, wg, conv_w, conv_b, rope_tabs, *, n_ctx_tok, ctx_len, lat_len):
    n_tok, D = x.shape
    tm, tn = TOKEN_TILE, COL_TILE
    n_ctx_tiles = n_ctx_tok // tm
    assert lat_len == tm and tm % ctx_len == 0 and n_ctx_tok % tm == 0

    def mod_row(i):
        return jnp.where(i < n_ctx_tiles, 0, i - n_ctx_tiles + 1)

    def conv_col(j):
        return jnp.clip(j - ZJ_MQ, 0, 1)

    kern = functools.partial(_inproj_kernel, n_ctx_tiles=n_ctx_tiles, ctx_len=ctx_len, lat_len=lat_len)
    return pl.pallas_call(
        kern,
        out_shape=(jax.ShapeDtypeStruct((n_tok, N_ZJ * tn), BF16),
                   jax.ShapeDtypeStruct((n_tok, 128), F32)),
        grid=(n_tok // tm, N_ZJ),
        in_specs=[
            pl.BlockSpec((tm, D), lambda i, j: (i, 0)),
            pl.BlockSpec((1, 1, 2 * D), lambda i, j: (mod_row(i), 0, 0)),
            pl.BlockSpec((D, tn), lambda i, j: (0, j)),
            pl.BlockSpec((D, 128), lambda i, j: (0, 0)),
            pl.BlockSpec((3, tn), lambda i, j: (0, conv_col(j))),
            pl.BlockSpec((1, tn), lambda i, j: (0, conv_col(j))),
            pl.BlockSpec((tm, 128), lambda i, j: (0, 0)),
            pl.BlockSpec((tm, 128), lambda i, j: (0, 0)),
            pl.BlockSpec((tm, 128), lambda i, j: (0, 0)),
        ],
        out_specs=(pl.BlockSpec((tm, tn), lambda i, j: (i, j)),
                   pl.BlockSpec((tm, 128), lambda i, j: (i, 0))),
        scratch_shapes=[pltpu.VMEM((tm, D), BF16)],
        compiler_params=_cparams(("arbitrary", "arbitrary")),
        name="inproj",
    )(x, mod_l, wz, wg, conv_w, conv_b, *rope_tabs)


def _diff_attn_core(q, ks, vs, lam, gain):
    lane = lax.broadcasted_iota(jnp.int32, (1, 128), 1)
    lo = lane < DA_HD
    zero = jnp.zeros_like(q)
    q1 = jnp.where(lo, q, zero)
    q2 = jnp.where(lo, zero, q)

    def unnormalised(qh):
        s = [_dot_nt(qh, k) for k in ks]
        mx = functools.reduce(jnp.maximum, [jnp.max(t, axis=-1, keepdims=True) for t in s])
        e = [jnp.exp(t - mx) for t in s]
        den = functools.reduce(jnp.add, [jnp.sum(t, axis=-1, keepdims=True) for t in e])
        return e, 1.0 / den

    e1, r1 = unnormalised(q1)
    e2, r2 = unnormalised(q2)
    r2 = lam * r2
    o = None
    for a1, a2, v in zip(e1, e2, vs):
        a = (a1 * r1 - a2 * r2).astype(BF16)
        t = _dot(a, v)
        o = t if o is None else o + t
    return _rms(o) * gain


def _lambda_full(lam_ref, lam_init):
    p = lam_ref[...]
    s1 = jnp.sum(p[0:1, :] * p[1:2, :], axis=-1, keepdims=True)
    s2 = jnp.sum(p[2:3, :] * p[3:4, :], axis=-1, keepdims=True)
    return jnp.exp(s1) - jnp.exp(s2) + lam_init


def _attn_ctx_kernel(q_ref, k_ref, v_ref, lam_ref, g_ref, y_ref, ck_ref, cv_ref, *, lam_init):
    lam = _lambda_full(lam_ref, lam_init)
    k = k_ref[...]
    v = v_ref[...]
    o = _diff_attn_core(q_ref[...], [k], [v], lam, g_ref[...])
    y_ref[...] = (o * (1.0 - lam_init)).astype(BF16)
    ck_ref[0, 0] = k.astype(F32)
    cv_ref[0, 0] = v.astype(F32)


def _attn_ctx(z, da_lam, norm_g, *, n_seq, seq_len, lam_init):
    hb = COL_TILE // 128
    kern = functools.partial(_attn_ctx_kernel, lam_init=lam_init)
    blk = lambda off: pl.BlockSpec((seq_len, 128), lambda b, h: (b, off * hb + h))
    return pl.pallas_call(
        kern,
        out_shape=(jax.ShapeDtypeStruct((n_seq * seq_len, BR_W), BF16),
                   jax.ShapeDtypeStruct((n_seq, DA_HEADS, seq_len, 128), F32),
                   jax.ShapeDtypeStruct((n_seq, DA_HEADS, seq_len, 128), F32)),
        grid=(n_seq, DA_HEADS),
        in_specs=[blk(ZJ_Q), blk(ZJ_K), blk(ZJ_V),
                  pl.BlockSpec((4, DA_HD), lambda b, h: (0, 0)),
                  pl.BlockSpec((1, 128), lambda b, h: (0, 0))],
        out_specs=(pl.BlockSpec((seq_len, 128), lambda b, h: (b, h)),
                   pl.BlockSpec((1, 1, seq_len, 128), lambda b, h: (b, h, 0, 0)),
                   pl.BlockSpec((1, 1, seq_len, 128), lambda b, h: (b, h, 0, 0))),
        compiler_params=_cparams(("arbitrary", "arbitrary")),
        name="attn_ctx",
    )(z, z, z, da_lam, norm_g)


def _attn_lat_kernel(q_ref, k_ref, v_ref, ck_ref, cv_ref, lam_ref, g_ref, y_ref, *, lam_init):
    lam = _lambda_full(lam_ref, lam_init)
    ks = [ck_ref[0, 0, 0].astype(BF16), k_ref[...]]
    vs = [cv_ref[0, 0, 0].astype(BF16), v_ref[...]]
    o = _diff_attn_core(q_ref[...], ks, vs, lam, g_ref[...])
    y_ref[...] = (o * (1.0 - lam_init)).astype(BF16)


def _attn_lat(z, cache_k, cache_v, da_lam, norm_g, *, layer, n_ctx_tok, n_seq, seq_len, lam_init):
    hb = COL_TILE // 128
    nq = seq_len // Q_TILE
    q0 = n_ctx_tok // Q_TILE
    s0 = n_ctx_tok // seq_len
    past = cache_k.shape[3]
    kern = functools.partial(_attn_lat_kernel, lam_init=lam_init)
    return pl.pallas_call(
        kern,
        out_shape=jax.ShapeDtypeStruct((n_seq * seq_len, BR_W), BF16),
        grid=(n_seq, DA_HEADS, nq),
        in_specs=[
            pl.BlockSpec((Q_TILE, 128), lambda b, h, t: (q0 + b * nq + t, ZJ_Q * hb + h)),
            pl.BlockSpec((seq_len, 128), lambda b, h, t: (s0 + b, ZJ_K * hb + h)),
            pl.BlockSpec((seq_len, 128), lambda b, h, t: (s0 + b, ZJ_V * hb + h)),
            pl.BlockSpec((1, 1, 1, past, 128), lambda b, h, t: (b, layer, h, 0, 0)),
            pl.BlockSpec((1, 1, 1, past, 128), lambda b, h, t: (b, layer, h, 0, 0)),
            pl.BlockSpec((4, DA_HD), lambda b, h, t: (0, 0)),
            pl.BlockSpec((1, 128), lambda b, h, t: (0, 0)),
        ],
        out_specs=pl.BlockSpec((Q_TILE, 128), lambda b, h, t: (b * nq + t, h)),
        compiler_params=_cparams(("arbitrary", "arbitrary", "arbitrary")),
        name="attn_lat",
    )(z, z, z, cache_k, cache_v, da_lam, norm_g)


def _split3(x):
    hi = x.astype(BF16)
    r = x - hi.astype(F32)
    mid = r.astype(BF16)
    lo = (r - mid.astype(F32)).astype(BF16)
    return hi, mid, lo


def _mlstm_kernel(*refs, n_chunks, has_init, emit_state):
    q_ref, k_ref, v_ref, og_ref, gt_ref, gb_ref, ng_ref = refs[:7]
    pos = 7
    if has_init:
        c0_ref, n0_ref, m0_ref = refs[pos:pos + 3]
        pos += 3
    y_ref = refs[pos]
    pos += 1
    if emit_state:
        co_ref, no_ref, mo_ref = refs[pos:pos + 3]
        pos += 3
    hacc, c_s, n_s, m_s = refs[pos:pos + 4]

    n_streams = 2 * ML_HEADS
    hacc[...] = jnp.zeros_like(hacc)
    for s in range(n_streams):
        d, h = divmod(s, ML_HEADS)
        if has_init:
            c_s[s] = c0_ref[0, 0, d, h]
            n_s[s] = n0_ref[0, 0, d, h:h + 1, :]
            m_s[s] = m0_ref[0, s:s + 1, :]
        else:
            c_s[s] = jnp.zeros((ML_HD, ML_HD), F32)
            n_s[s] = jnp.zeros((1, ML_HD), F32)
            m_s[s] = jnp.zeros((1, 128), F32)

    row = lax.broadcasted_iota(jnp.int32, (CHUNK, CHUNK), 0)
    col = lax.broadcasted_iota(jnp.int32, (CHUNK, CHUNK), 1)
    masks = (col <= row, col >= row)
    tris = tuple(jnp.where(m, 1.0, 0.0).astype(BF16) for m in masks)
    last_row = (CHUNK - 1, 0)

    def step(c_fw):
        for d in range(2):
            c = c_fw if d == 0 else n_chunks - 1 - c_fw
            r0 = c * CHUNK
            if not isinstance(r0, int):
                r0 = pl.multiple_of(r0, CHUNK)
            rows = pl.ds(r0, CHUNK)
            pre = gt_ref[rows, :] + gb_ref[...]
            logf = jax.nn.log_sigmoid(pre)
            hi, mid, lo = _split3(logf)
            csum = _dot(tris[d], hi) + _dot(tris[d], mid) + _dot(tris[d], lo)
            pre_t = pre.T
            csum_t = csum.T
            lr = last_row[d]
            for h in range(ML_HEADS):
                s = d * ML_HEADS + h
                li, lf = d * ML_HEADS + h, 2 * ML_HEADS + d * ML_HEADS + h
                i_col = pre[:, li:li + 1]
                b_col = csum[:, lf:lf + 1]
                i_row = pre_t[li:li + 1, :]
                b_row = csum_t[lf:lf + 1, :]
                m_prev = m_s[s][:, 0:1]
                c_prev = c_s[s]
                n_prev = n_s[s]
                hs = slice(h * ML_HD, (h + 1) * ML_HD)
                qc = q_ref[rows, hs]
                kc = k_ref[rows, hs]
                vc = v_ref[rows, hs]

                log_w = jnp.where(masks[d], b_col - b_row + i_row, NEG)
                inter = b_col + m_prev
                m_t = jnp.maximum(inter, jnp.max(log_w, axis=-1, keepdims=True))
                w = jnp.exp(log_w - m_t)
                s_inter = jnp.exp(inter - m_t)
                qk = _dot_nt(qc, kc) * w
                num = _dot(qk.astype(BF16), vc) + s_inter * _dot(qc, c_prev.astype(BF16))
                qn = jnp.sum(qc.astype(F32) * n_prev, axis=-1, keepdims=True)
                den = jnp.sum(qk, axis=-1, keepdims=True) + s_inter * qn
                hh = num / jnp.maximum(jnp.abs(den), jnp.exp(-m_t))
                hacc[rows, hs] += hh

                m_new = m_t[lr:lr + 1, :]
                b_last = b_col[lr:lr + 1, :]
                g = jnp.exp(b_last - b_col + i_col - m_new)
                decay = jnp.exp(b_last + m_prev - m_new)
                gk = g * kc.astype(F32)
                c_s[s] = decay * c_prev + _dot_tn(gk.astype(BF16), vc)
                n_s[s] = decay * n_prev + jnp.sum(gk, axis=0, keepdims=True)
                m_s[s] = jnp.broadcast_to(m_new, (1, 128))

    if n_chunks <= 2:
        for c in range(n_chunks):
            step(c)
    else:
        def body(c, carry):
            step(c)
            return carry
        lax.fori_loop(0, n_chunks, body, 0)

    for c in range(n_chunks):
        rows = slice(c * CHUNK, (c + 1) * CHUNK)
        for h in range(ML_HEADS):
            hs = slice(h * ML_HD, (h + 1) * ML_HD)
            hn = _rms(hacc[rows, hs]) * ng_ref[...]
            y_ref[rows, hs] = (og_ref[rows, hs].astype(F32) * hn.astype(F32)).astype(BF16)

    if emit_state:
        for s in range(n_streams):
            co_ref[0, s] = c_s[s]
            no_ref[0, s:s + 1, :] = n_s[s]
            mo_ref[0, s:s + 1, :] = m_s[s]


def _mlstm(z, gates, gate_b, norm_g, init, *, layer, tok0, n_seq, seq_len, emit_state):
    hb = COL_TILE // 128
    wb = BR_W // 128 * 128
    s0 = tok0 // seq_len
    n_chunks = seq_len // CHUNK
    has_init = init is not None
    n_streams = 2 * ML_HEADS
    zblk = lambda zj: pl.BlockSpec((seq_len, wb), lambda b: (s0 + b, zj))
    in_specs = [zblk(ZJ_MQ), zblk(ZJ_MK), zblk(ZJ_MV), zblk(ZJ_MO),
                pl.BlockSpec((seq_len, 128), lambda b: (s0 + b, 0)),
                pl.BlockSpec((1, 128), lambda b: (0, 0)),
                pl.BlockSpec((1, 128), lambda b: (0, 0))]
    args = [z, z, z, z, gates, gate_b, norm_g]
    if has_init:
        c0, n0, m0 = init
        in_specs += [
            pl.BlockSpec((1, 1, 2, ML_HEADS, ML_HD, ML_HD), lambda b: (b, layer, 0, 0, 0, 0)),
            pl.BlockSpec((1, 1, 2, ML_HEADS, ML_HD), lambda b: (b, layer, 0, 0, 0)),
            pl.BlockSpec((1, n_streams, 128), lambda b: (b, 0, 0)),
        ]
        args += [c0, n0, m0]
    out_shape = [jax.ShapeDtypeStruct((n_seq * seq_len, BR_W), BF16)]
    out_specs = [pl.BlockSpec((seq_len, BR_W), lambda b: (b, 0))]
    if emit_state:
        out_shape += [jax.ShapeDtypeStruct((n_seq, n_streams, ML_HD, ML_HD), F32),
                      jax.ShapeDtypeStruct((n_seq, n_streams, ML_HD), F32),
                      jax.ShapeDtypeStruct((n_seq, n_streams, 128), F32)]
        out_specs += [pl.BlockSpec((1, n_streams, ML_HD, ML_HD), lambda b: (b, 0, 0, 0)),
                      pl.BlockSpec((1, n_streams, ML_HD), lambda b: (b, 0, 0)),
                      pl.BlockSpec((1, n_streams, 128), lambda b: (b, 0, 0))]
    kern = functools.partial(_mlstm_kernel, n_chunks=n_chunks, has_init=has_init, emit_state=emit_state)
    return pl.pallas_call(
        kern,
        out_shape=tuple(out_shape),
        grid=(n_seq,),
        in_specs=in_specs,
        out_specs=tuple(out_specs),
        scratch_shapes=[pltpu.VMEM((seq_len, BR_W), F32),
                        pltpu.VMEM((n_streams, ML_HD, ML_HD), F32),
                        pltpu.VMEM((n_streams, 1, ML_HD), F32),
                        pltpu.VMEM((n_streams, 1, 128), F32)],
        compiler_params=_cparams(("arbitrary",)),
        name="mlstm_init" if has_init else "mlstm_zero",
    )(*args)


def _merge_kernel(x_ref, mod_ref, yda_ref, yml_ref, sgu_ref, sgv_ref, g0_ref, g1_ref, g2_ref,
                  sgn_ref, sgw_ref, sgb_ref, wb_ref, wo_ref, o_ref):
    D = x_ref.shape[1]
    tm = x_ref.shape[0]
    v = sgv_ref[...].astype(F32)
    vc = v - jnp.mean(v, axis=-1, keepdims=True)
    sv = (vc * lax.rsqrt(jnp.mean(vc * vc, axis=-1, keepdims=True) + EPS) * sgn_ref[...]).astype(BF16)
    mixed = []
    for c in range(tm // CHUNK):
        rows = slice(c * CHUNK, (c + 1) * CHUNK)
        groups = [_dot(sgw_ref[g], sv[rows, g * 128:(g + 1) * 128]) for g in range(SG_GROUPS)]
        mixed.append(jnp.concatenate(groups, axis=1) + sgb_ref[...])
    y_sg = (sgu_ref[...].astype(F32) * jnp.concatenate(mixed, axis=0)).astype(BF16)

    m = g0_ref[...].astype(F32) * _dot(yda_ref[...], wb_ref[0])
    m = m + g1_ref[...].astype(F32) * _dot(yml_ref[...], wb_ref[1])
    m = m + g2_ref[...].astype(F32) * _dot(y_sg, wb_ref[2])
    out = _dot(m.astype(BF16), wo_ref[...])
    o_ref[...] = x_ref[...] + mod_ref[0, :, 2 * D:3 * D] * out


def _mod_row_fn(tile, n_ctx_tok, lat_len):
    def f(i):
        t0 = i * tile
        return jnp.where(t0 < n_ctx_tok, 0, 1 + (t0 - n_ctx_tok) // lat_len)
    return f


def _merge(x, mod_l, y_da, y_ml, z, sg_norm_g, sg_w, sg_bias, w_branch, w_out, *, n_ctx_tok, lat_len):
    n_tok, D = x.shape
    tm = MERGE_TILE
    mrow = _mod_row_fn(tm, n_ctx_tok, lat_len)
    gate_blk = lambda n: pl.BlockSpec((tm, D), lambda i: (i, ZJ_GATE0 * COL_TILE // D + n))
    return pl.pallas_call(
        _merge_kernel,
        out_shape=jax.ShapeDtypeStruct((n_tok, D), F32),
        grid=(n_tok // tm,),
        in_specs=[
            pl.BlockSpec((tm, D), lambda i: (i, 0)),
            pl.BlockSpec((1, 1, 6 * D), lambda i: (mrow(i), 0, 0)),
            pl.BlockSpec((tm, BR_W), lambda i: (i, 0)),
            pl.BlockSpec((tm, BR_W), lambda i: (i, 0)),
            pl.BlockSpec((tm, COL_TILE), lambda i: (i, ZJ_SGU)),
            pl.BlockSpec((tm, COL_TILE), lambda i: (i, ZJ_SGV)),
            gate_blk(0), gate_blk(1), gate_blk(2),
            pl.BlockSpec((1, BR_W), lambda i: (0, 0)),
            pl.BlockSpec((SG_GROUPS, CHUNK, CHUNK), lambda i: (0, 0, 0)),
            pl.BlockSpec((CHUNK, BR_W), lambda i: (0, 0)),
            pl.BlockSpec((N_BRANCH, BR_W, D), lambda i: (0, 0, 0)),
            pl.BlockSpec((D, D), lambda i: (0, 0)),
        ],
        out_specs=pl.BlockSpec((tm, D), lambda i: (i, 0)),
        compiler_params=_cparams(("arbitrary",)),
        name="merge",
    )(x, mod_l, y_da, y_ml, z, z, z, z, z, sg_norm_g, sg_w, sg_bias, w_branch, w_out)


def _ffn_kernel(x_ref, mod_ref, wa_ref, wg_ref, cwa_ref, cwg_ref, cba_ref, cbg_ref, wd_ref, fg_ref,
                o_ref, h_scr, acc, *, n_ctx_tiles, ctx_len, lat_len, final_norm):
    i = pl.program_id(0)
    c = pl.program_id(1)
    D = x_ref.shape[1]

    @pl.when(c == 0)
    def _():
        h = _rms(x_ref[...]) * (1.0 + mod_ref[0, :, 4 * D:5 * D]) + mod_ref[0, :, 3 * D:4 * D]
        h_scr[...] = h.astype(BF16)
        acc[...] = jnp.zeros_like(acc)

    seq_len = jnp.where(i >= n_ctx_tiles, lat_len, ctx_len)
    h = h_scr[...]
    ua = _seq_conv3(_dot(h, wa_ref[...]), cwa_ref, cba_ref, seq_len)
    ug = _seq_conv3(_dot(h, wg_ref[...]), cwg_ref, cbg_ref, seq_len)
    a = (jax.nn.silu(ua) * ug).astype(BF16)
    acc[...] += _dot(a, wd_ref[...])

    @pl.when(c == pl.num_programs(1) - 1)
    def _():
        y = x_ref[...] + mod_ref[0, :, 5 * D:6 * D] * acc[...]
        if final_norm:
            y = _rms(y) * fg_ref[...]
        o_ref[...] = y


def _ffn(x, mod_l, w_up, conv_w, conv_b, w_down, final_g, *, n_ctx_tok, ctx_len, lat_len, final_norm):
    n_tok, D = x.shape
    d_ff = w_down.shape[0]
    tm, tc = TOKEN_TILE, FF_TILE
    nc = d_ff // tc
    n_ctx_tiles = n_ctx_tok // tm
    mrow = _mod_row_fn(tm, n_ctx_tok, lat_len)
    kern = functools.partial(_ffn_kernel, n_ctx_tiles=n_ctx_tiles, ctx_len=ctx_len, lat_len=lat_len,
                             final_norm=final_norm)
    return pl.pallas_call(
        kern,
        out_shape=jax.ShapeDtypeStruct((n_tok, D), F32),
        grid=(n_tok // tm, nc),
        in_specs=[
            pl.BlockSpec((tm, D), lambda i, c: (i, 0)),
            pl.BlockSpec((1, 1, 6 * D), lambda i, c: (mrow(i), 0, 0)),
            pl.BlockSpec((D, tc), lambda i, c: (0, c)),
            pl.BlockSpec((D, tc), lambda i, c: (0, nc + c)),
            pl.BlockSpec((3, tc), lambda i, c: (0, c)),
            pl.BlockSpec((3, tc), lambda i, c: (0, nc + c)),
            pl.BlockSpec((1, tc), lambda i, c: (0, c)),
            pl.BlockSpec((1, tc), lambda i, c: (0, nc + c)),
            pl.BlockSpec((tc, D), lambda i, c: (c, 0)),
            pl.BlockSpec((1, D), lambda i, c: (0, 0)),
        ],
        out_specs=pl.BlockSpec((tm, D), lambda i, c: (i, 0)),
        scratch_shapes=[pltpu.VMEM((tm, D), BF16), pltpu.VMEM((tm, D), F32)],
        compiler_params=_cparams(("arbitrary", "arbitrary")),
        name="ffn_final" if final_norm else "ffn",
    )(x, mod_l, w_up, w_up, conv_w, conv_w, conv_b, conv_b, w_down, final_g)


def _rope_tables(n_pos):
    t = jnp.arange(n_pos)
    nf = DA_HD // 4
    inv = ROPE_THETA ** (-jnp.arange(nf, dtype=F32) / nf)
    ang = [(t // GRID_W).astype(F32)[:, None] * inv, (t % GRID_W).astype(F32)[:, None] * inv]
    zeros = jnp.zeros((n_pos, nf), F32)

    def lanes(first, second):
        sub = jnp.concatenate([first(ang[0]), second(ang[0]), first(ang[1]), second(ang[1])], axis=1)
        return jnp.concatenate([sub, sub], axis=1)

    cos = lanes(jnp.cos, jnp.cos)
    sin_a = lanes(lambda a: -jnp.sin(a), lambda a: zeros)
    sin_b = lanes(lambda a: zeros, jnp.sin)
    return cos, sin_a, sin_b


def kernel(x_prompt, x_sample, c, cache_k, cache_v, state_C, state_n, state_m, c_ctx, w_mod, b_mod, w_in,
           da_lambda, da_norm_g, ml_conv_w, ml_conv_b, ml_gate_b, ml_norm_g, sg_norm_g, sg_w, sg_b,
           w_branch, w_out, w_up, ffn_conv_w, ffn_conv_b, w_down, final_g):
    B, S, D = x_prompt.shape
    Bd, Sd, _ = x_sample.shape
    L = w_mod.shape[0]
    n_ctx_tok = B * S
    n_streams = 2 * ML_HEADS

    x = jnp.concatenate([x_prompt.reshape(n_ctx_tok, D), x_sample.reshape(Bd * Sd, D)], axis=0)
    cond8 = jnp.concatenate([c_ctx[None, :], c, jnp.zeros((8 - 1 - Bd, D), F32)], axis=0)
    mod = _modulation(cond8, w_mod, b_mod)
    rope_tabs = _rope_tables(Sd)

    da_w = DA_HEADS * 2 * DA_HD
    o_mlg = 3 * da_w + 4 * BR_W
    o_sg = o_mlg + 4 * ML_HEADS
    o_gate = o_sg + 2 * BR_W

    new_k, new_v, new_C, new_n, new_m = [], [], [], [], []
    for l in range(L):
        wl = w_in[l]
        wz = jnp.concatenate([wl[:, :o_mlg], wl[:, o_sg:o_sg + BR_W], wl[:, o_gate:],
                              wl[:, o_sg + BR_W:o_gate]], axis=1).astype(BF16)
        wg = jnp.pad(wl[:, o_mlg:o_sg], ((0, 0), (0, 128 - 4 * ML_HEADS))).astype(BF16)
        mod_l = mod[l].reshape(8, 1, 6 * D)
        lam_init = 0.8 - 0.6 * math.exp(-0.3 * l)

        z, gates = _inproj(x, mod_l[:, :, :2 * D], wz, wg, ml_conv_w[l], ml_conv_b[l].reshape(1, -1),
                           rope_tabs, n_ctx_tok=n_ctx_tok, ctx_len=S, lat_len=Sd)

        da_g = da_norm_g[l].reshape(1, -1)
        y_da_c, k_l, v_l = _attn_ctx(z, da_lambda[l], da_g, n_seq=B, seq_len=S, lam_init=lam_init)
        y_da_s = _attn_lat(z, cache_k, cache_v, da_lambda[l], da_g, layer=l, n_ctx_tok=n_ctx_tok,
                           n_seq=Bd, seq_len=Sd, lam_init=lam_init)

        gate_b = jnp.pad(ml_gate_b[l].reshape(1, -1), ((0, 0), (0, 128 - 4 * ML_HEADS)))
        ml_g = ml_norm_g[l].reshape(1, -1)
        y_ml_c, C_l, n_l, m_l = _mlstm(z, gates, gate_b, ml_g, None, layer=l, tok0=0, n_seq=B,
                                       seq_len=S, emit_state=True)
        m0 = jnp.broadcast_to(state_m[:, l].reshape(Bd, n_streams, 1), (Bd, n_streams, 128))
        (y_ml_s,) = _mlstm(z, gates, gate_b, ml_g, (state_C, state_n, m0), layer=l, tok0=n_ctx_tok,
                           n_seq=Bd, seq_len=Sd, emit_state=False)

        y_da = jnp.concatenate([y_da_c, y_da_s], axis=0)
        y_ml = jnp.concatenate([y_ml_c, y_ml_s], axis=0)
        sg_bias = jnp.repeat(sg_b[l].T, BR_W // SG_GROUPS, axis=1)
        x = _merge(x, mod_l, y_da, y_ml, z, sg_norm_g[l].reshape(1, -1), sg_w[l].astype(BF16), sg_bias,
                   w_branch[l].astype(BF16), w_out[l].astype(BF16), n_ctx_tok=n_ctx_tok, lat_len=Sd)
        x = _ffn(x, mod_l, w_up[l].astype(BF16), ffn_conv_w[l], ffn_conv_b[l].reshape(1, -1),
                 w_down[l].astype(BF16), final_g.reshape(1, -1), n_ctx_tok=n_ctx_tok, ctx_len=S,
                 lat_len=Sd, final_norm=(l == L - 1))

        new_k.append(k_l)
        new_v.append(v_l)
        new_C.append(C_l.reshape(B, 2, ML_HEADS, ML_HD, ML_HD))
        new_n.append(n_l.reshape(B, 2, ML_HEADS, ML_HD))
        new_m.append(m_l[:, :, 0].reshape(B, 2, ML_HEADS))

    y_prompt = x[:n_ctx_tok].reshape(B, S, D)
    y_sample = x[n_ctx_tok:].reshape(Bd, Sd, D)
    return (y_prompt, y_sample, jnp.stack(new_k, axis=1), jnp.stack(new_v, axis=1),
            jnp.stack(new_C, axis=1), jnp.stack(new_n, axis=1), jnp.stack(new_m, axis=1))
```

```python
import functools
import math

import jax
import jax.numpy as jnp
from jax import lax
from jax.experimental import pallas as pl
from jax.experimental.pallas import tpu as pltpu

F32 = jnp.float32
BF16 = jnp.bfloat16

GRID_W = 64
DA_HEADS = 4
DA_HD = 64
ML_HEADS = 4
ML_HD = 128
CHUNK = 128
SG_GROUPS = 4
BR_W = 512
N_BRANCH = 3
ROPE_THETA = 10000.0
EPS = 1e-6
NEG = -1e30

VMEM_LIMIT_BYTES = 52 * 1024 * 1024

TOKEN_TILE = 1024
COL_TILE = 512
MERGE_TILE = 512
FF_TILE = 256
Q_TILE = 256

ZJ_Q, ZJ_K, ZJ_V, ZJ_MQ, ZJ_MK, ZJ_MV, ZJ_MO, ZJ_SGU = range(8)
ZJ_GATE0 = 8
ZJ_SGV = 14
N_ZJ = 15


def _cparams(sem):
    return pltpu.CompilerParams(dimension_semantics=sem, vmem_limit_bytes=VMEM_LIMIT_BYTES)


def _dot(a, b):
    return jnp.dot(a, b, preferred_element_type=F32)


def _dot_nt(a, b):
    return lax.dot_general(a, b, (((1,), (1,)), ((), ())), preferred_element_type=F32)


def _dot_tn(a, b):
    return lax.dot_general(a, b, (((0,), (0,)), ((), ())), preferred_element_type=F32)


def _mod_kernel(cond_ref, w_ref, b_ref, o_ref):
    a = jax.nn.silu(cond_ref[...]).astype(BF16)
    o_ref[0] = _dot(a, w_ref[0].astype(BF16)) + b_ref[0]


def _modulation(cond8, w_mod, b_mod):
    L, D, N = w_mod.shape
    tn = 768
    return pl.pallas_call(
        _mod_kernel,
        out_shape=jax.ShapeDtypeStruct((L, 8, N), F32),
        grid=(L, N // tn),
        in_specs=[
            pl.BlockSpec((8, D), lambda l, j: (0, 0)),
            pl.BlockSpec((1, D, tn), lambda l, j: (l, 0, j)),
            pl.BlockSpec((1, 1, tn), lambda l, j: (l, 0, j)),
        ],
        out_specs=pl.BlockSpec((1, 8, tn), lambda l, j: (l, 0, j)),
        compiler_params=_cparams(("arbitrary", "arbitrary")),
        name="modulation",
    )(cond8, w_mod, b_mod.reshape(L, 1, N))


def _rms(x):
    return x * lax.rsqrt(jnp.mean(x * x, axis=-1, keepdims=True) + EPS)


def _seq_conv3(u, w_ref, b_ref, seq_len):
    rows = u.shape[0]
    pos = lax.broadcasted_iota(jnp.int32, (rows, 1), 0) & (seq_len - 1)
    has_prev = (pos != 0).astype(F32)
    has_next = (pos != seq_len - 1).astype(F32)
    prev = pltpu.roll(u, 1, axis=0) * has_prev
    nxt = pltpu.roll(u, rows - 1, axis=0) * has_next
    return prev * w_ref[0, 0:1, :] + u * w_ref[0, 1:2, :] + nxt * w_ref[0, 2:3, :] + b_ref[0]


def _inproj_kernel(x_ref, mod_ref, wz_ref, wg_ref, cw_ref, cb_ref, cos_ref, sa_ref, sb_ref,
                   z_ref, g_ref, h_scr, *, n_ctx_tiles, ctx_len, lat_len):
    i = pl.program_id(0)
    j = pl.program_id(1)
    D = x_ref.shape[1]

    @pl.when(j == 0)
    def _():
        h = _rms(x_ref[...]) * (1.0 + mod_ref[0, :, D:2 * D]) + mod_ref[0, :, 0:D]
        hb = h.astype(BF16)
        h_scr[...] = hb
        g_ref[...] = _dot(hb, wg_ref[0])

    z = _dot(h_scr[...], wz_ref[0])
    is_lat = i >= n_ctx_tiles

    def rope(zz):
        parts = []
        for hd in range(zz.shape[1] // 128):
            xh = zz[:, hd * 128:(hd + 1) * 128]
            parts.append(xh * cos_ref[...] + pltpu.roll(xh, 112, axis=1) * sa_ref[...]
                         + pltpu.roll(xh, 16, axis=1) * sb_ref[...])
        return jnp.concatenate(parts, axis=1)

    @pl.when((j == ZJ_Q) & is_lat)
    def _():
        z_ref[...] = (rope(z) * (DA_HD ** -0.5)).astype(BF16)

    @pl.when((j == ZJ_Q) & jnp.logical_not(is_lat))
    def _():
        z_ref[...] = (z * (DA_HD ** -0.5)).astype(BF16)

    @pl.when((j == ZJ_K) & is_lat)
    def _():
        z_ref[...] = rope(z).astype(BF16)

    @pl.when(((j == ZJ_K) & jnp.logical_not(is_lat)) | (j == ZJ_V) | (j == ZJ_MV))
    def _():
        z_ref[...] = z.astype(BF16)

    @pl.when((j == ZJ_MQ) | (j == ZJ_MK))
    def _():
        seq_len = jnp.where(is_lat, lat_len, ctx_len)
        a = jax.nn.silu(_seq_conv3(z, cw_ref, cb_ref, seq_len))
        scale = jnp.where(j == ZJ_MK, ML_HD ** -0.5, 1.0).astype(F32)
        z_ref[...] = (a * scale).astype(BF16)

    @pl.when((j == ZJ_MO) | ((j >= ZJ_GATE0) & (j < ZJ_SGV)))
    def _():
        z_ref[...] = jax.nn.sigmoid(z).astype(BF16)

    @pl.when((j == ZJ_SGU) | (j == ZJ_SGV))
    def _():
        z_ref[...] = jax.nn.gelu(z, approximate=True).astype(BF16)


def _inproj(x, mod, wz, wg, conv_w, conv_b, rope_tabs, *, layer, n_ctx_tok, ctx_len, lat_len):
    n_tok, D = x.shape
    tm, tn = TOKEN_TILE, COL_TILE
    n_ctx_tiles = n_ctx_tok // tm
    assert lat_len == tm and tm % ctx_len == 0 and n_ctx_tok % tm == 0

    def mod_row(i):
        return layer * 8 + jnp.where(i < n_ctx_tiles, 0, i - n_ctx_tiles + 1)

    def conv_col(j):
        return jnp.clip(j - ZJ_MQ, 0, 1)

    kern = functools.partial(_inproj_kernel, n_ctx_tiles=n_ctx_tiles, ctx_len=ctx_len, lat_len=lat_len)
    return pl.pallas_call(
        kern,
        out_shape=(jax.ShapeDtypeStruct((n_tok, N_ZJ * tn), BF16),
                   jax.ShapeDtypeStruct((n_tok, 128), F32)),
        grid=(n_tok // tm, N_ZJ),
        in_specs=[
            pl.BlockSpec((tm, D), lambda i, j: (i, 0)),
            pl.BlockSpec((1, 1, 2 * D), lambda i, j: (mod_row(i), 0, 0)),
            pl.BlockSpec((1, D, tn), lambda i, j: (layer, 0, j)),
            pl.BlockSpec((1, D, 128), lambda i, j: (layer, 0, 0)),
            pl.BlockSpec((1, 3, tn), lambda i, j: (layer, 0, conv_col(j))),
            pl.BlockSpec((1, 1, tn), lambda i, j: (layer, 0, conv_col(j))),
            pl.BlockSpec((tm, 128), lambda i, j: (0, 0)),
            pl.BlockSpec((tm, 128), lambda i, j: (0, 0)),
            pl.BlockSpec((tm, 128), lambda i, j: (0, 0)),
        ],
        out_specs=(pl.BlockSpec((tm, tn), lambda i, j: (i, j)),
                   pl.BlockSpec((tm, 128), lambda i, j: (i, 0))),
        scratch_shapes=[pltpu.VMEM((tm, D), BF16)],
        compiler_params=_cparams(("arbitrary", "arbitrary")),
        name="inproj",
    )(x, mod, wz, wg, conv_w, conv_b, *rope_tabs)


def _diff_attn_core(q, ks, vs, lam, gain):
    lane = lax.broadcasted_iota(jnp.int32, (1, 128), 1)
    lo = lane < DA_HD
    zero = jnp.zeros_like(q)
    q1 = jnp.where(lo, q, zero)
    q2 = jnp.where(lo, zero, q)

    def unnormalised(qh):
        s = [_dot_nt(qh, k) for k in ks]
        mx = functools.reduce(jnp.maximum, [jnp.max(t, axis=-1, keepdims=True) for t in s])
        e = [jnp.exp(t - mx) for t in s]
        den = functools.reduce(jnp.add, [jnp.sum(t, axis=-1, keepdims=True) for t in e])
        return e, 1.0 / den

    e1, r1 = unnormalised(q1)
    e2, r2 = unnormalised(q2)
    r2 = lam * r2
    o = None
    for a1, a2, v in zip(e1, e2, vs):
        a = (a1 * r1 - a2 * r2).astype(BF16)
        t = _dot(a, v)
        o = t if o is None else o + t
    return _rms(o) * gain


def _lambda_full(lam_ref, lam_init):
    p = lam_ref[0]
    s1 = jnp.sum(p[0:1, :] * p[1:2, :], axis=-1, keepdims=True)
    s2 = jnp.sum(p[2:3, :] * p[3:4, :], axis=-1, keepdims=True)
    return jnp.exp(s1) - jnp.exp(s2) + lam_init


def _carried(arrays):
    return [pl.BlockSpec(memory_space=pl.ANY)] * len(arrays)


def _attn_ctx_kernel(*refs, lam_init, n_carried):
    q_ref, k_ref, v_ref, lam_ref, g_ref = refs[:5]
    y_ref, ck_ref, cv_ref = refs[5 + n_carried:]
    lam = _lambda_full(lam_ref, lam_init)
    k = k_ref[...]
    v = v_ref[...]
    o = _diff_attn_core(q_ref[...], [k], [v], lam, g_ref[0])
    y_ref[...] = (o * (1.0 - lam_init)).astype(BF16)
    ck_ref[0, 0, 0] = k.astype(F32)
    cv_ref[0, 0, 0] = v.astype(F32)


def _attn_ctx(z, da_lam, norm_g, carried, *, layer, n_layers, n_seq, seq_len, lam_init):
    hb = COL_TILE // 128
    n_tok = z.shape[0]
    kern = functools.partial(_attn_ctx_kernel, lam_init=lam_init, n_carried=len(carried))
    blk = lambda off: pl.BlockSpec((seq_len, 128), lambda b, h: (b, off * hb + h))
    cache_sds = jax.ShapeDtypeStruct((n_seq, n_layers, DA_HEADS, seq_len, 128), F32)
    cache_blk = pl.BlockSpec((1, 1, 1, seq_len, 128), lambda b, h: (b, layer, h, 0, 0))
    return pl.pallas_call(
        kern,
        out_shape=(jax.ShapeDtypeStruct((n_tok, BR_W), BF16), cache_sds, cache_sds),
        grid=(n_seq, DA_HEADS),
        in_specs=[blk(ZJ_Q), blk(ZJ_K), blk(ZJ_V),
                  pl.BlockSpec((1, 4, DA_HD), lambda b, h: (layer, 0, 0)),
                  pl.BlockSpec((1, 1, 128), lambda b, h: (layer, 0, 0))] + _carried(carried),
        out_specs=(pl.BlockSpec((seq_len, 128), lambda b, h: (b, h)), cache_blk, cache_blk),
        input_output_aliases={5 + n: 1 + n for n in range(len(carried))},
        compiler_params=_cparams(("arbitrary", "arbitrary")),
        name="attn_ctx",
    )(z, z, z, da_lam, norm_g, *carried)


def _attn_lat_kernel(q_ref, k_ref, v_ref, ck_ref, cv_ref, lam_ref, g_ref, yin_ref, y_ref, *, lam_init):
    del yin_ref
    lam = _lambda_full(lam_ref, lam_init)
    ks = [ck_ref[0, 0, 0].astype(BF16), k_ref[...]]
    vs = [cv_ref[0, 0, 0].astype(BF16), v_ref[...]]
    o = _diff_attn_core(q_ref[...], ks, vs, lam, g_ref[0])
    y_ref[...] = (o * (1.0 - lam_init)).astype(BF16)


def _attn_lat(z, cache_k, cache_v, da_lam, norm_g, y_da, *, layer, n_ctx_tok, n_seq, seq_len, lam_init):
    hb = COL_TILE // 128
    nq = seq_len // Q_TILE
    q0 = n_ctx_tok // Q_TILE
    s0 = n_ctx_tok // seq_len
    past = cache_k.shape[3]
    kern = functools.partial(_attn_lat_kernel, lam_init=lam_init)
    return pl.pallas_call(
        kern,
        out_shape=jax.ShapeDtypeStruct(y_da.shape, y_da.dtype),
        grid=(n_seq, DA_HEADS, nq),
        in_specs=[
            pl.BlockSpec((Q_TILE, 128), lambda b, h, t: (q0 + b * nq + t, ZJ_Q * hb + h)),
            pl.BlockSpec((seq_len, 128), lambda b, h, t: (s0 + b, ZJ_K * hb + h)),
            pl.BlockSpec((seq_len, 128), lambda b, h, t: (s0 + b, ZJ_V * hb + h)),
            pl.BlockSpec((1, 1, 1, past, 128), lambda b, h, t: (b, layer, h, 0, 0)),
            pl.BlockSpec((1, 1, 1, past, 128), lambda b, h, t: (b, layer, h, 0, 0)),
            pl.BlockSpec((1, 4, DA_HD), lambda b, h, t: (layer, 0, 0)),
            pl.BlockSpec((1, 1, 128), lambda b, h, t: (layer, 0, 0)),
            pl.BlockSpec(memory_space=pl.ANY),
        ],
        out_specs=pl.BlockSpec((Q_TILE, 128), lambda b, h, t: (q0 + b * nq + t, h)),
        input_output_aliases={7: 0},
        compiler_params=_cparams(("arbitrary", "arbitrary", "arbitrary")),
        name="attn_lat",
    )(z, z, z, cache_k, cache_v, da_lam, norm_g, y_da)


def _split3(x):
    hi = x.astype(BF16)
    r = x - hi.astype(F32)
    mid = r.astype(BF16)
    lo = (r - mid.astype(F32)).astype(BF16)
    return hi, mid, lo


def _mlstm_kernel(*refs, n_chunks, has_init, emit_state, n_carried):
    q_ref, k_ref, v_ref, og_ref, gt_ref, gb_ref, ng_ref = refs[:7]
    pos = 7
    if has_init:
        c0_ref, n0_ref, m0_ref = refs[pos:pos + 3]
        pos += 3
    pos += n_carried
    y_ref = refs[pos]
    pos += 1
    if emit_state:
        co_ref, no_ref, mo_ref = refs[pos:pos + 3]
        pos += 3
    hacc, c_s, n_s, m_s = refs[pos:pos + 4]

    n_streams = 2 * ML_HEADS
    hacc[...] = jnp.zeros_like(hacc)
    for s in range(n_streams):
        d, h = divmod(s, ML_HEADS)
        if has_init:
            c_s[s] = c0_ref[0, 0, d, h]
            n_s[s] = n0_ref[0, 0, d, h:h + 1, :]
            m_s[s] = m0_ref[0, s:s + 1, :]
        else:
            c_s[s] = jnp.zeros((ML_HD, ML_HD), F32)
            n_s[s] = jnp.zeros((1, ML_HD), F32)
            m_s[s] = jnp.zeros((1, 128), F32)

    row = lax.broadcasted_iota(jnp.int32, (CHUNK, CHUNK), 0)
    col = lax.broadcasted_iota(jnp.int32, (CHUNK, CHUNK), 1)
    masks = (col <= row, col >= row)
    tris = tuple(jnp.where(m, 1.0, 0.0).astype(BF16) for m in masks)
    last_row = (CHUNK - 1, 0)

    def step(c_fw):
        for d in range(2):
            c = c_fw if d == 0 else n_chunks - 1 - c_fw
            r0 = c * CHUNK
            if not isinstance(r0, int):
                r0 = pl.multiple_of(r0, CHUNK)
            rows = pl.ds(r0, CHUNK)
            pre = gt_ref[rows, :] + gb_ref[0]
            logf = jax.nn.log_sigmoid(pre)
            hi, mid, lo = _split3(logf)
            csum = _dot(tris[d], hi) + _dot(tris[d], mid) + _dot(tris[d], lo)
            pre_t = pre.T
            csum_t = csum.T
            lr = last_row[d]
            for h in range(ML_HEADS):
                s = d * ML_HEADS + h
                li, lf = d * ML_HEADS + h, 2 * ML_HEADS + d * ML_HEADS + h
                i_col = pre[:, li:li + 1]
                b_col = csum[:, lf:lf + 1]
                i_row = pre_t[li:li + 1, :]
                b_row = csum_t[lf:lf + 1, :]
                m_prev = m_s[s][:, 0:1]
                c_prev = c_s[s]
                n_prev = n_s[s]
                hs = slice(h * ML_HD, (h + 1) * ML_HD)
                qc = q_ref[rows, hs]
                kc = k_ref[rows, hs]
                vc = v_ref[rows, hs]

                log_w = jnp.where(masks[d], b_col - b_row + i_row, NEG)
                inter = b_col + m_prev
                m_t = jnp.maximum(inter, jnp.max(log_w, axis=-1, keepdims=True))
                w = jnp.exp(log_w - m_t)
                s_inter = jnp.exp(inter - m_t)
                qk = _dot_nt(qc, kc) * w
                num = _dot(qk.astype(BF16), vc) + s_inter * _dot(qc, c_prev.astype(BF16))
                qn = jnp.sum(qc.astype(F32) * n_prev, axis=-1, keepdims=True)
                den = jnp.sum(qk, axis=-1, keepdims=True) + s_inter * qn
                hh = num / jnp.maximum(jnp.abs(den), jnp.exp(-m_t))
                hacc[rows, hs] += hh

                m_new = m_t[lr:lr + 1, :]
                b_last = b_col[lr:lr + 1, :]
                g = jnp.exp(b_last - b_col + i_col - m_new)
                decay = jnp.exp(b_last + m_prev - m_new)
                gk = g * kc.astype(F32)
                c_s[s] = decay * c_prev + _dot_tn(gk.astype(BF16), vc)
                n_s[s] = decay * n_prev + jnp.sum(gk, axis=0, keepdims=True)
                m_s[s] = jnp.broadcast_to(m_new, (1, 128))

    if n_chunks <= 2:
        for c in range(n_chunks):
            step(c)
    else:
        def body(c, carry):
            step(c)
            return carry
        lax.fori_loop(0, n_chunks, body, 0)

    for c in range(n_chunks):
        rows = slice(c * CHUNK, (c + 1) * CHUNK)
        for h in range(ML_HEADS):
            hs = slice(h * ML_HD, (h + 1) * ML_HD)
            hn = _rms(hacc[rows, hs]) * ng_ref[0]
            y_ref[rows, hs] = (og_ref[rows, hs].astype(F32) * hn.astype(F32)).astype(BF16)

    if emit_state:
        for s in range(n_streams):
            co_ref[0, 0, s] = c_s[s]
            no_ref[0, 0, s:s + 1, :] = n_s[s]
            mo_ref[0, 0, s:s + 1, :] = m_s[s]


def _mlstm(z, gates, gate_b, norm_g, init, carried, *, layer, n_layers, tok0, n_seq, seq_len, emit_state):
    n_tok = z.shape[0]
    s0 = tok0 // seq_len
    n_chunks = seq_len // CHUNK
    has_init = init is not None
    n_streams = 2 * ML_HEADS
    zblk = lambda zj: pl.BlockSpec((seq_len, BR_W), lambda b: (s0 + b, zj))
    in_specs = [zblk(ZJ_MQ), zblk(ZJ_MK), zblk(ZJ_MV), zblk(ZJ_MO),
                pl.BlockSpec((seq_len, 128), lambda b: (s0 + b, 0)),
                pl.BlockSpec((1, 1, 128), lambda b: (layer, 0, 0)),
                pl.BlockSpec((1, 1, 128), lambda b: (layer, 0, 0))]
    args = [z, z, z, z, gates, gate_b, norm_g]
    if has_init:
        c0, n0, m0 = init
        in_specs += [
            pl.BlockSpec((1, 1, 2, ML_HEADS, ML_HD, ML_HD), lambda b: (b, layer, 0, 0, 0, 0)),
            pl.BlockSpec((1, 1, 2, ML_HEADS, ML_HD), lambda b: (b, layer, 0, 0, 0)),
            pl.BlockSpec((1, n_streams, 128), lambda b: (b, 0, 0)),
        ]
        args += [c0, n0, m0]
    first_carried = len(args)
    in_specs += _carried(carried)
    args += list(carried)
    out_shape = [jax.ShapeDtypeStruct((n_tok, BR_W), BF16)]
    out_specs = [pl.BlockSpec((seq_len, BR_W), lambda b: (s0 + b, 0))]
    if emit_state:
        out_shape += [jax.ShapeDtypeStruct((n_seq, n_layers, n_streams, ML_HD, ML_HD), F32),
                      jax.ShapeDtypeStruct((n_seq, n_layers, n_streams, ML_HD), F32),
                      jax.ShapeDtypeStruct((n_seq, n_layers, n_streams, 128), F32)]
        out_specs += [pl.BlockSpec((1, 1, n_streams, ML_HD, ML_HD), lambda b: (b, layer, 0, 0, 0)),
                      pl.BlockSpec((1, 1, n_streams, ML_HD), lambda b: (b, layer, 0, 0)),
                      pl.BlockSpec((1, 1, n_streams, 128), lambda b: (b, layer, 0, 0))]
        aliases = {first_carried + n: 1 + n for n in range(len(carried))}
    else:
        aliases = {first_carried: 0}
    kern = functools.partial(_mlstm_kernel, n_chunks=n_chunks, has_init=has_init, emit_state=emit_state,
                             n_carried=len(carried))
    return pl.pallas_call(
        kern,
        out_shape=tuple(out_shape),
        grid=(n_seq,),
        in_specs=in_specs,
        out_specs=tuple(out_specs),
        input_output_aliases=aliases,
        scratch_shapes=[pltpu.VMEM((seq_len, BR_W), F32),
                        pltpu.VMEM((n_streams, ML_HD, ML_HD), F32),
                        pltpu.VMEM((n_streams, 1, ML_HD), F32),
                        pltpu.VMEM((n_streams, 1, 128), F32)],
        compiler_params=_cparams(("arbitrary",)),
        name="mlstm_init" if has_init else "mlstm_zero",
    )(*args)


def _merge_kernel(x_ref, mod_ref, yda_ref, yml_ref, sgu_ref, sgv_ref, g0_ref, g1_ref, g2_ref,
                  sgn_ref, sgw_ref, sgb_ref, wb_ref, wo_ref, o_ref):
    D = x_ref.shape[1]
    tm = x_ref.shape[0]
    v = sgv_ref[...].astype(F32)
    vc = v - jnp.mean(v, axis=-1, keepdims=True)
    sv = (vc * lax.rsqrt(jnp.mean(vc * vc, axis=-1, keepdims=True) + EPS) * sgn_ref[0]).astype(BF16)
    mixed = []
    for c in range(tm // CHUNK):
        rows = slice(c * CHUNK, (c + 1) * CHUNK)
        groups = [_dot(sgw_ref[0, g], sv[rows, g * 128:(g + 1) * 128]) for g in range(SG_GROUPS)]
        mixed.append(jnp.concatenate(groups, axis=1) + sgb_ref[0])
    y_sg = (sgu_ref[...].astype(F32) * jnp.concatenate(mixed, axis=0)).astype(BF16)

    m = g0_ref[...].astype(F32) * _dot(yda_ref[...], wb_ref[0, 0])
    m = m + g1_ref[...].astype(F32) * _dot(yml_ref[...], wb_ref[0, 1])
    m = m + g2_ref[...].astype(F32) * _dot(y_sg, wb_ref[0, 2])
    out = _dot(m.astype(BF16), wo_ref[0])
    o_ref[...] = x_ref[...] + mod_ref[0, :, 2 * D:3 * D] * out


def _mod_row_fn(layer, tile, n_ctx_tok, lat_len):
    def f(i):
        t0 = i * tile
        return layer * 8 + jnp.where(t0 < n_ctx_tok, 0, 1 + (t0 - n_ctx_tok) // lat_len)
    return f


def _merge(x, mod, y_da, y_ml, z, sg_norm_g, sg_w, sg_bias, w_branch, w_out, *, layer, n_ctx_tok, lat_len):
    n_tok, D = x.shape
    tm = MERGE_TILE
    mrow = _mod_row_fn(layer, tm, n_ctx_tok, lat_len)
    gate_blk = lambda n: pl.BlockSpec((tm, D), lambda i: (i, ZJ_GATE0 * COL_TILE // D + n))
    return pl.pallas_call(
        _merge_kernel,
        out_shape=jax.ShapeDtypeStruct((n_tok, D), F32),
        grid=(n_tok // tm,),
        in_specs=[
            pl.BlockSpec((tm, D), lambda i: (i, 0)),
            pl.BlockSpec((1, 1, 6 * D), lambda i: (mrow(i), 0, 0)),
            pl.BlockSpec((tm, BR_W), lambda i: (i, 0)),
            pl.BlockSpec((tm, BR_W), lambda i: (i, 0)),
            pl.BlockSpec((tm, COL_TILE), lambda i: (i, ZJ_SGU)),
            pl.BlockSpec((tm, COL_TILE), lambda i: (i, ZJ_SGV)),
            gate_blk(0), gate_blk(1), gate_blk(2),
            pl.BlockSpec((1, 1, BR_W), lambda i: (layer, 0, 0)),
            pl.BlockSpec((1, SG_GROUPS, CHUNK, CHUNK), lambda i: (layer, 0, 0, 0)),
            pl.BlockSpec((1, CHUNK, BR_W), lambda i: (layer, 0, 0)),
            pl.BlockSpec((1, N_BRANCH, BR_W, D), lambda i: (layer, 0, 0, 0)),
            pl.BlockSpec((1, D, D), lambda i: (layer, 0, 0)),
        ],
        out_specs=pl.BlockSpec((tm, D), lambda i: (i, 0)),
        compiler_params=_cparams(("arbitrary",)),
        name="merge",
    )(x, mod, y_da, y_ml, z, z, z, z, z, sg_norm_g, sg_w, sg_bias, w_branch, w_out)


def _ffn_kernel(x_ref, mod_ref, wa_ref, wg_ref, cwa_ref, cwg_ref, cba_ref, cbg_ref, wd_ref, fg_ref,
                *rest, n_ctx_tiles, ctx_len, lat_len, final_norm):
    out_refs, (h_scr, acc) = rest[:-2], rest[-2:]
    i = pl.program_id(0)
    c = pl.program_id(1)
    D = x_ref.shape[1]
    is_lat = i >= n_ctx_tiles

    @pl.when(c == 0)
    def _():
        h = _rms(x_ref[...]) * (1.0 + mod_ref[0, :, 4 * D:5 * D]) + mod_ref[0, :, 3 * D:4 * D]
        h_scr[...] = h.astype(BF16)
        acc[...] = jnp.zeros_like(acc)

    seq_len = jnp.where(is_lat, lat_len, ctx_len)
    h = h_scr[...]
    ua = _seq_conv3(_dot(h, wa_ref[0].astype(BF16)), cwa_ref, cba_ref, seq_len)
    ug = _seq_conv3(_dot(h, wg_ref[0].astype(BF16)), cwg_ref, cbg_ref, seq_len)
    a = (jax.nn.silu(ua) * ug).astype(BF16)
    acc[...] += _dot(a, wd_ref[0].astype(BF16))

    last = c == pl.num_programs(1) - 1
    if final_norm:
        yp_ref, ys_ref = out_refs

        def result():
            return _rms(x_ref[...] + mod_ref[0, :, 5 * D:6 * D] * acc[...]) * fg_ref[...]

        @pl.when(last & jnp.logical_not(is_lat))
        def _():
            yp_ref[...] = result()

        @pl.when(last & is_lat)
        def _():
            ys_ref[...] = result()
    else:
        @pl.when(last)
        def _():
            out_refs[0][...] = x_ref[...] + mod_ref[0, :, 5 * D:6 * D] * acc[...]


def _ffn(x, mod, w_up, conv_w, conv_b, w_down, final_g, *, layer, n_ctx_tok, ctx_len, lat_len, final_norm):
    n_tok, D = x.shape
    d_ff = w_down.shape[1]
    tm, tc = TOKEN_TILE, FF_TILE
    nc = d_ff // tc
    n_ctx_tiles = n_ctx_tok // tm
    mrow = _mod_row_fn(layer, tm, n_ctx_tok, lat_len)
    kern = functools.partial(_ffn_kernel, n_ctx_tiles=n_ctx_tiles, ctx_len=ctx_len, lat_len=lat_len,
                             final_norm=final_norm)
    if final_norm:
        out_shape = (jax.ShapeDtypeStruct((n_ctx_tok, D), F32), jax.ShapeDtypeStruct((n_tok - n_ctx_tok, D), F32))
        out_specs = (pl.BlockSpec((tm, D), lambda i, c: (jnp.minimum(i, n_ctx_tiles - 1), 0)),
                     pl.BlockSpec((tm, D), lambda i, c: (jnp.maximum(i - n_ctx_tiles, 0), 0)))
    else:
        out_shape = jax.ShapeDtypeStruct((n_tok, D), F32)
        out_specs = pl.BlockSpec((tm, D), lambda i, c: (i, 0))
    return pl.pallas_call(
        kern,
        out_shape=out_shape,
        grid=(n_tok // tm, nc),
        in_specs=[
            pl.BlockSpec((tm, D), lambda i, c: (i, 0)),
            pl.BlockSpec((1, 1, 6 * D), lambda i, c: (mrow(i), 0, 0)),
            pl.BlockSpec((1, D, tc), lambda i, c: (layer, 0, c)),
            pl.BlockSpec((1, D, tc), lambda i, c: (layer, 0, nc + c)),
            pl.BlockSpec((1, 3, tc), lambda i, c: (layer, 0, c)),
            pl.BlockSpec((1, 3, tc), lambda i, c: (layer, 0, nc + c)),
            pl.BlockSpec((1, 1, tc), lambda i, c: (layer, 0, c)),
            pl.BlockSpec((1, 1, tc), lambda i, c: (layer, 0, nc + c)),
            pl.BlockSpec((1, tc, D), lambda i, c: (layer, c, 0)),
            pl.BlockSpec((1, D), lambda i, c: (0, 0)),
        ],
        out_specs=out_specs,
        scratch_shapes=[pltpu.VMEM((tm, D), BF16), pltpu.VMEM((tm, D), F32)],
        compiler_params=_cparams(("arbitrary", "arbitrary")),
        name="ffn_final" if final_norm else "ffn",
    )(x, mod, w_up, w_up, conv_w, conv_w, conv_b, conv_b, w_down, final_g)


def _rope_tables(n_pos):
    t = jnp.arange(n_pos)
    nf = DA_HD // 4
    inv = ROPE_THETA ** (-jnp.arange(nf, dtype=F32) / nf)
    ang = [(t // GRID_W).astype(F32)[:, None] * inv, (t % GRID_W).astype(F32)[:, None] * inv]
    zeros = jnp.zeros((n_pos, nf), F32)

    def lanes(first, second):
        sub = jnp.concatenate([first(ang[0]), second(ang[0]), first(ang[1]), second(ang[1])], axis=1)
        return jnp.concatenate([sub, sub], axis=1)

    cos = lanes(jnp.cos, jnp.cos)
    sin_a = lanes(lambda a: -jnp.sin(a), lambda a: zeros)
    sin_b = lanes(lambda a: zeros, jnp.sin)
    return cos, sin_a, sin_b


def kernel(x_prompt, x_sample, c, cache_k, cache_v, state_C, state_n, state_m, c_ctx, w_mod, b_mod, w_in,
           da_lambda, da_norm_g, ml_conv_w, ml_conv_b, ml_gate_b, ml_norm_g, sg_norm_g, sg_w, sg_b,
           w_branch, w_out, w_up, ffn_conv_w, ffn_conv_b, w_down, final_g):
    B, S, D = x_prompt.shape
    Bd, Sd, _ = x_sample.shape
    L = w_mod.shape[0]
    n_ctx_tok = B * S
    n_streams = 2 * ML_HEADS

    x = jnp.concatenate([x_prompt.reshape(n_ctx_tok, D), x_sample.reshape(Bd * Sd, D)], axis=0)
    cond8 = jnp.concatenate([c_ctx[None, :], c, jnp.zeros((8 - 1 - Bd, D), F32)], axis=0)
    mod = _modulation(cond8, w_mod, b_mod).reshape(L * 8, 1, 6 * D)
    rope_tabs = _rope_tables(Sd)

    da_w = DA_HEADS * 2 * DA_HD
    o_mlg = 3 * da_w + 4 * BR_W
    o_sg = o_mlg + 4 * ML_HEADS
    o_gate = o_sg + 2 * BR_W
    wz = jnp.concatenate([w_in[:, :, :o_mlg], w_in[:, :, o_sg:o_sg + BR_W], w_in[:, :, o_gate:],
                          w_in[:, :, o_sg + BR_W:o_gate]], axis=2).astype(BF16)
    lane_pad = 128 - 4 * ML_HEADS
    wg = jnp.pad(w_in[:, :, o_mlg:o_sg], ((0, 0), (0, 0), (0, lane_pad))).astype(BF16)
    gate_b = jnp.pad(ml_gate_b.reshape(L, 1, -1), ((0, 0), (0, 0), (0, lane_pad)))
    sg_bias = jnp.repeat(jnp.swapaxes(sg_b, 1, 2), BR_W // SG_GROUPS, axis=2)
    sg_w16 = sg_w.astype(BF16)
    w_branch16 = w_branch.astype(BF16)
    w_out16 = w_out.astype(BF16)
    row = lambda p: p.reshape(L, 1, -1)

    caches, states = (), ()
    for l in range(L):
        lam_init = 0.8 - 0.6 * math.exp(-0.3 * l)
        z, gates = _inproj(x, mod, wz, wg, ml_conv_w, row(ml_conv_b), rope_tabs, layer=l,
                           n_ctx_tok=n_ctx_tok, ctx_len=S, lat_len=Sd)

        y_da, *caches = _attn_ctx(z, da_lambda, row(da_norm_g), tuple(caches), layer=l, n_layers=L,
                                  n_seq=B, seq_len=S, lam_init=lam_init)
        y_da = _attn_lat(z, cache_k, cache_v, da_lambda, row(da_norm_g), y_da, layer=l,
                         n_ctx_tok=n_ctx_tok, n_seq=Bd, seq_len=Sd, lam_init=lam_init)

        y_ml, *states = _mlstm(z, gates, gate_b, row(ml_norm_g), None, tuple(states), layer=l, n_layers=L,
                               tok0=0, n_seq=B, seq_len=S, emit_state=True)
        m0 = jnp.broadcast_to(state_m[:, l].reshape(Bd, n_streams, 1), (Bd, n_streams, 128))
        (y_ml,) = _mlstm(z, gates, gate_b, row(ml_norm_g), (state_C, state_n, m0), (y_ml,), layer=l,
                         n_layers=L, tok0=n_ctx_tok, n_seq=Bd, seq_len=Sd, emit_state=False)

        x = _merge(x, mod, y_da, y_ml, z, row(sg_norm_g), sg_w16, sg_bias, w_branch16, w_out16,
                   layer=l, n_ctx_tok=n_ctx_tok, lat_len=Sd)
        x = _ffn(x, mod, w_up, ffn_conv_w, row(ffn_conv_b), w_down, final_g.reshape(1, -1), layer=l,
                 n_ctx_tok=n_ctx_tok, ctx_len=S, lat_len=Sd, final_norm=(l == L - 1))

    y_prompt, y_sample = x
    new_k, new_v = caches
    new_C, new_n, new_m = states
    return (y_prompt.reshape(B, S, D), y_sample.reshape(Bd, Sd, D), new_k, new_v,
            new_C.reshape(B, L, 2, ML_HEADS, ML_HD, ML_HD), new_n.reshape(B, L, 2, ML_HEADS, ML_HD),
            new_m[:, :, :, 0].reshape(B, L, 2, ML_HEADS))
```

```python
import functools
import math

import jax
import jax.numpy as jnp
from jax import lax
from jax.experimental import pallas as pl
from jax.experimental.pallas import tpu as pltpu

F32 = jnp.float32
BF16 = jnp.bfloat16

GRID_W = 64
DA_HEADS = 4
DA_HD = 64
ML_HEADS = 4
ML_HD = 128
CHUNK = 128
SG_GROUPS = 4
BR_W = 512
N_BRANCH = 3
ROPE_THETA = 10000.0
EPS = 1e-6
NEG = -1e30

VMEM_LIMIT_BYTES = 52 * 1024 * 1024

TOKEN_TILE = 1024
COL_TILE = 512
MERGE_TILE = 512
FF_TILE = 256
Q_TILE = 256

ZJ_Q, ZJ_K, ZJ_V, ZJ_MQ, ZJ_MK, ZJ_MV, ZJ_MO, ZJ_SGU = range(8)
ZJ_GATE0 = 8
ZJ_SGV = 14
N_ZJ = 15


def _cparams(sem):
    return pltpu.CompilerParams(dimension_semantics=sem, vmem_limit_bytes=VMEM_LIMIT_BYTES)


def _dot(a, b):
    return jnp.dot(a, b, preferred_element_type=F32)


def _dot_nt(a, b):
    return lax.dot_general(a, b, (((1,), (1,)), ((), ())), preferred_element_type=F32)


def _dot_tn(a, b):
    return lax.dot_general(a, b, (((0,), (0,)), ((), ())), preferred_element_type=F32)


def _mod_kernel(cond_ref, w_ref, b_ref, o_ref):
    a = jax.nn.silu(cond_ref[...]).astype(BF16)
    o_ref[0] = _dot(a, w_ref[0].astype(BF16)) + b_ref[0]


def _modulation(cond8, w_mod, b_mod):
    L, D, N = w_mod.shape
    tn = 768
    return pl.pallas_call(
        _mod_kernel,
        out_shape=jax.ShapeDtypeStruct((L, 8, N), F32),
        grid=(L, N // tn),
        in_specs=[
            pl.BlockSpec((8, D), lambda l, j: (0, 0)),
            pl.BlockSpec((1, D, tn), lambda l, j: (l, 0, j)),
            pl.BlockSpec((1, 1, tn), lambda l, j: (l, 0, j)),
        ],
        out_specs=pl.BlockSpec((1, 8, tn), lambda l, j: (l, 0, j)),
        compiler_params=_cparams(("arbitrary", "arbitrary")),
        name="modulation",
    )(cond8, w_mod, b_mod.reshape(L, 1, N))


def _rms(x):
    return x * lax.rsqrt(jnp.mean(x * x, axis=-1, keepdims=True) + EPS)


def _seq_conv3(u, w_ref, b_ref, is_lat, ctx_len):
    rows = u.shape[0]
    prev = pltpu.roll(u, 1, axis=0)
    nxt = pltpu.roll(u, rows - 1, axis=0)
    r8 = lax.broadcasted_iota(jnp.int32, (8, 1), 0)
    interior = jnp.where(is_lat, 1.0, 0.0).astype(F32)
    n_pieces = rows // ctx_len
    pp, nn = [], []
    for p in range(n_pieces):
        lo, hi = p * ctx_len, (p + 1) * ctx_len
        keep_first = interior if p > 0 else 0.0
        keep_last = interior if p < n_pieces - 1 else 0.0
        pp += [prev[lo:lo + 8] * jnp.where(r8 == 0, keep_first, 1.0), prev[lo + 8:hi]]
        nn += [nxt[lo:hi - 8], nxt[hi - 8:hi] * jnp.where(r8 == 7, keep_last, 1.0)]
    prev = jnp.concatenate(pp, axis=0)
    nxt = jnp.concatenate(nn, axis=0)
    return prev * w_ref[0, 0:1, :] + u * w_ref[0, 1:2, :] + nxt * w_ref[0, 2:3, :] + b_ref[0]


def _inproj_kernel(x_ref, mod_ref, wz_ref, wg_ref, cw_ref, cb_ref, cos_ref, sa_ref, sb_ref,
                   z_ref, g_ref, kt_ref, h_scr, *, n_ctx_tiles, ctx_len, lat_len):
    i = pl.program_id(0)
    j = pl.program_id(1)
    D = x_ref.shape[1]

    @pl.when(j == 0)
    def _():
        h = _rms(x_ref[...]) * (1.0 + mod_ref[0, :, D:2 * D]) + mod_ref[0, :, 0:D]
        hb = h.astype(BF16)
        h_scr[...] = hb
        g_ref[...] = _dot(hb, wg_ref[0])

    z = _dot(h_scr[...], wz_ref[0])
    is_lat = i >= n_ctx_tiles

    def rope(zz):
        parts = []
        for hd in range(zz.shape[1] // 128):
            xh = zz[:, hd * 128:(hd + 1) * 128]
            parts.append(xh * cos_ref[...] + pltpu.roll(xh, 112, axis=1) * sa_ref[...]
                         + pltpu.roll(xh, 16, axis=1) * sb_ref[...])
        return jnp.concatenate(parts, axis=1)

    @pl.when((j == ZJ_Q) & is_lat)
    def _():
        z_ref[...] = (rope(z) * (DA_HD ** -0.5)).astype(BF16)

    @pl.when((j == ZJ_Q) & jnp.logical_not(is_lat))
    def _():
        z_ref[...] = (z * (DA_HD ** -0.5)).astype(BF16)

    @pl.when((j == ZJ_K) & is_lat)
    def _():
        z_ref[...] = rope(z).astype(BF16)

    @pl.when(((j == ZJ_K) & jnp.logical_not(is_lat)) | (j == ZJ_V) | (j == ZJ_MV))
    def _():
        z_ref[...] = z.astype(BF16)

    @pl.when(j == ZJ_MQ)
    def _():
        z_ref[...] = jax.nn.silu(_seq_conv3(z, cw_ref, cb_ref, is_lat, ctx_len)).astype(BF16)

    @pl.when(j == ZJ_MK)
    def _():
        a = jax.nn.silu(_seq_conv3(z, cw_ref, cb_ref, is_lat, ctx_len)) * (ML_HD ** -0.5)
        kt_ref[...] = a.T.astype(BF16)

    @pl.when((j == ZJ_MO) | ((j >= ZJ_GATE0) & (j < ZJ_SGV)))
    def _():
        z_ref[...] = jax.nn.sigmoid(z).astype(BF16)

    @pl.when((j == ZJ_SGU) | (j == ZJ_SGV))
    def _():
        z_ref[...] = jax.nn.gelu(z, approximate=True).astype(BF16)


def _inproj(x, mod, wz, wg, conv_w, conv_b, rope_tabs, *, layer, n_ctx_tok, ctx_len, lat_len):
    n_tok, D = x.shape
    tm, tn = TOKEN_TILE, COL_TILE
    n_ctx_tiles = n_ctx_tok // tm
    assert lat_len == tm and tm % ctx_len == 0 and n_ctx_tok % tm == 0

    def mod_row(i):
        return layer * 8 + jnp.where(i < n_ctx_tiles, 0, i - n_ctx_tiles + 1)

    def conv_col(j):
        return jnp.clip(j - ZJ_MQ, 0, 1)

    kern = functools.partial(_inproj_kernel, n_ctx_tiles=n_ctx_tiles, ctx_len=ctx_len, lat_len=lat_len)
    return pl.pallas_call(
        kern,
        out_shape=(jax.ShapeDtypeStruct((n_tok, N_ZJ * tn), BF16),
                   jax.ShapeDtypeStruct((n_tok, 128), F32),
                   jax.ShapeDtypeStruct((tn, n_tok), BF16)),
        grid=(n_tok // tm, N_ZJ),
        in_specs=[
            pl.BlockSpec((tm, D), lambda i, j: (i, 0)),
            pl.BlockSpec((1, 1, 2 * D), lambda i, j: (mod_row(i), 0, 0)),
            pl.BlockSpec((1, D, tn), lambda i, j: (layer, 0, j)),
            pl.BlockSpec((1, D, 128), lambda i, j: (layer, 0, 0)),
            pl.BlockSpec((1, 3, tn), lambda i, j: (layer, 0, conv_col(j))),
            pl.BlockSpec((1, 1, tn), lambda i, j: (layer, 0, conv_col(j))),
            pl.BlockSpec((tm, 128), lambda i, j: (0, 0)),
            pl.BlockSpec((tm, 128), lambda i, j: (0, 0)),
            pl.BlockSpec((tm, 128), lambda i, j: (0, 0)),
        ],
        out_specs=(pl.BlockSpec((tm, tn), lambda i, j: (i, jnp.where(j == ZJ_MK, ZJ_MQ, j))),
                   pl.BlockSpec((tm, 128), lambda i, j: (i, 0)),
                   pl.BlockSpec((tn, tm), lambda i, j: (0, i))),
        scratch_shapes=[pltpu.VMEM((tm, D), BF16)],
        compiler_params=_cparams(("arbitrary", "arbitrary")),
        name="inproj",
    )(x, mod, wz, wg, conv_w, conv_b, *rope_tabs)


def _diff_attn_core(q, ks, vs, lam, gain):
    lane = lax.broadcasted_iota(jnp.int32, (1, 128), 1)
    lo = lane < DA_HD
    zero = jnp.zeros_like(q)
    q1 = jnp.where(lo, q, zero)
    q2 = jnp.where(lo, zero, q)

    def unnormalised(qh):
        s = [_dot_nt(qh, k) for k in ks]
        mx = functools.reduce(jnp.maximum, [jnp.max(t, axis=-1, keepdims=True) for t in s])
        e = [jnp.exp(t - mx) for t in s]
        den = functools.reduce(jnp.add, [jnp.sum(t, axis=-1, keepdims=True) for t in e])
        return e, 1.0 / den

    e1, r1 = unnormalised(q1)
    e2, r2 = unnormalised(q2)
    r2 = lam * r2
    o = None
    for a1, a2, v in zip(e1, e2, vs):
        a = (a1 * r1 - a2 * r2).astype(BF16)
        t = _dot(a, v)
        o = t if o is None else o + t
    return _rms(o) * gain


def _lambda_full(lam_ref, lam_init):
    p = lam_ref[0]
    s1 = jnp.sum(p[0:1, :] * p[1:2, :], axis=-1, keepdims=True)
    s2 = jnp.sum(p[2:3, :] * p[3:4, :], axis=-1, keepdims=True)
    return jnp.exp(s1) - jnp.exp(s2) + lam_init


def _carried(arrays):
    return [pl.BlockSpec(memory_space=pl.ANY)] * len(arrays)


def _attn_ctx_kernel(*refs, lam_init, n_carried):
    q_ref, k_ref, v_ref, lam_ref, g_ref = refs[:5]
    y_ref, ck_ref, cv_ref = refs[5 + n_carried:]
    lam = _lambda_full(lam_ref, lam_init)
    k = k_ref[...]
    v = v_ref[...]
    o = _diff_attn_core(q_ref[...], [k], [v], lam, g_ref[0])
    y_ref[...] = (o * (1.0 - lam_init)).astype(BF16)
    ck_ref[0, 0, 0] = k.astype(F32)
    cv_ref[0, 0, 0] = v.astype(F32)


def _attn_ctx(z, da_lam, norm_g, carried, *, layer, n_layers, n_seq, seq_len, lam_init):
    hb = COL_TILE // 128
    n_tok = z.shape[0]
    kern = functools.partial(_attn_ctx_kernel, lam_init=lam_init, n_carried=len(carried))
    blk = lambda off: pl.BlockSpec((seq_len, 128), lambda b, h: (b, off * hb + h))
    cache_sds = jax.ShapeDtypeStruct((n_seq, n_layers, DA_HEADS, seq_len, 128), F32)
    cache_blk = pl.BlockSpec((1, 1, 1, seq_len, 128), lambda b, h: (b, layer, h, 0, 0))
    return pl.pallas_call(
        kern,
        out_shape=(jax.ShapeDtypeStruct((n_tok, BR_W), BF16), cache_sds, cache_sds),
        grid=(n_seq, DA_HEADS),
        in_specs=[blk(ZJ_Q), blk(ZJ_K), blk(ZJ_V),
                  pl.BlockSpec((1, 4, DA_HD), lambda b, h: (layer, 0, 0)),
                  pl.BlockSpec((1, 1, 128), lambda b, h: (layer, 0, 0))] + _carried(carried),
        out_specs=(pl.BlockSpec((seq_len, 128), lambda b, h: (b, h)), cache_blk, cache_blk),
        input_output_aliases={5 + n: 1 + n for n in range(len(carried))},
        compiler_params=_cparams(("arbitrary", "arbitrary")),
        name="attn_ctx",
    )(z, z, z, da_lam, norm_g, *carried)


def _attn_lat_kernel(q_ref, k_ref, v_ref, ck_ref, cv_ref, lam_ref, g_ref, yin_ref, y_ref, *, lam_init):
    del yin_ref
    lam = _lambda_full(lam_ref, lam_init)
    ks = [ck_ref[0, 0, 0].astype(BF16), k_ref[...]]
    vs = [cv_ref[0, 0, 0].astype(BF16), v_ref[...]]
    o = _diff_attn_core(q_ref[...], ks, vs, lam, g_ref[0])
    y_ref[...] = (o * (1.0 - lam_init)).astype(BF16)


def _attn_lat(z, cache_k, cache_v, da_lam, norm_g, y_da, *, layer, n_ctx_tok, n_seq, seq_len, lam_init):
    hb = COL_TILE // 128
    nq = seq_len // Q_TILE
    q0 = n_ctx_tok // Q_TILE
    s0 = n_ctx_tok // seq_len
    past = cache_k.shape[3]
    kern = functools.partial(_attn_lat_kernel, lam_init=lam_init)
    return pl.pallas_call(
        kern,
        out_shape=jax.ShapeDtypeStruct(y_da.shape, y_da.dtype),
        grid=(n_seq, DA_HEADS, nq),
        in_specs=[
            pl.BlockSpec((Q_TILE, 128), lambda b, h, t: (q0 + b * nq + t, ZJ_Q * hb + h)),
            pl.BlockSpec((seq_len, 128), lambda b, h, t: (s0 + b, ZJ_K * hb + h)),
            pl.BlockSpec((seq_len, 128), lambda b, h, t: (s0 + b, ZJ_V * hb + h)),
            pl.BlockSpec((1, 1, 1, past, 128), lambda b, h, t: (b, layer, h, 0, 0)),
            pl.BlockSpec((1, 1, 1, past, 128), lambda b, h, t: (b, layer, h, 0, 0)),
            pl.BlockSpec((1, 4, DA_HD), lambda b, h, t: (layer, 0, 0)),
            pl.BlockSpec((1, 1, 128), lambda b, h, t: (layer, 0, 0)),
            pl.BlockSpec(memory_space=pl.ANY),
        ],
        out_specs=pl.BlockSpec((Q_TILE, 128), lambda b, h, t: (q0 + b * nq + t, h)),
        input_output_aliases={7: 0},
        compiler_params=_cparams(("arbitrary", "arbitrary", "arbitrary")),
        name="attn_lat",
    )(z, z, z, cache_k, cache_v, da_lam, norm_g, y_da)


def _split3(x):
    hi = x.astype(BF16)
    r = x - hi.astype(F32)
    mid = r.astype(BF16)
    lo = (r - mid.astype(F32)).astype(BF16)
    return hi, mid, lo


def _mlstm_kernel(*refs, n_chunks, has_init, emit_state, n_carried):
    q_ref, kt_ref, v_ref, og_ref, gt_ref, gb_ref, ng_ref = refs[:7]
    pos = 7
    if has_init:
        c0_ref, n0_ref, m0_ref = refs[pos:pos + 3]
        pos += 3
    pos += n_carried
    y_ref = refs[pos]
    pos += 1
    if emit_state:
        co_ref, no_ref, mo_ref = refs[pos:pos + 3]
        pos += 3
    hacc, cn_s, m_s = refs[pos:pos + 3]

    n_streams = 2 * ML_HEADS
    hacc[...] = jnp.zeros_like(hacc)
    for s in range(n_streams):
        d, h = divmod(s, ML_HEADS)
        if has_init:
            n_col = jnp.broadcast_to(n0_ref[0, 0, d, h:h + 1, :], (ML_HD, ML_HD)).T
            cn_s[s] = jnp.concatenate([c0_ref[0, 0, d, h], n_col], axis=1)
            m_s[s] = m0_ref[0, s:s + 1, :]
        else:
            cn_s[s] = jnp.zeros((ML_HD, 2 * ML_HD), F32)
            m_s[s] = jnp.zeros((1, 128), F32)

    row = lax.broadcasted_iota(jnp.int32, (CHUNK, CHUNK), 0)
    col = lax.broadcasted_iota(jnp.int32, (CHUNK, CHUNK), 1)
    masks = (col <= row, col >= row)
    tris = tuple(jnp.where(m, 1.0, 0.0).astype(BF16) for m in masks)
    last_row = (CHUNK - 1, 0)

    def step(c_fw):
        for d in range(2):
            c = c_fw if d == 0 else n_chunks - 1 - c_fw
            r0 = c * CHUNK
            if not isinstance(r0, int):
                r0 = pl.multiple_of(r0, CHUNK)
            rows = pl.ds(r0, CHUNK)
            pre = gt_ref[rows, :] + gb_ref[0]
            logf = jax.nn.log_sigmoid(pre)
            hi, mid, lo = _split3(logf)
            csum = _dot(tris[d], hi) + _dot(tris[d], mid) + _dot(tris[d], lo)
            pre_t = pre.T
            csum_t = csum.T
            lr = last_row[d]
            ones = jnp.ones((CHUNK, ML_HD), BF16)
            for h in range(ML_HEADS):
                s = d * ML_HEADS + h
                li, lf = d * ML_HEADS + h, 2 * ML_HEADS + d * ML_HEADS + h
                b_t = jnp.broadcast_to(csum[:, lf:lf + 1], (CHUNK, 128))
                r_row = pre_t[li:li + 1, :] - csum_t[lf:lf + 1, :]
                m_prev = m_s[s]
                cn_prev = cn_s[s]
                hs = slice(h * ML_HD, (h + 1) * ML_HD)
                qc = q_ref[rows, hs]
                ktc = kt_ref[hs, rows]
                v_ext = jnp.concatenate([v_ref[rows, hs], ones], axis=1)

                log_w = jnp.where(masks[d], b_t + r_row, NEG)
                inter = b_t + m_prev
                m_t = jnp.maximum(inter, jnp.broadcast_to(jnp.max(log_w, axis=-1, keepdims=True),
                                                          (CHUNK, 128)))
                w = jnp.exp(log_w - m_t)
                s_inter = jnp.exp(inter - m_t)
                qk = (_dot(qc, ktc) * w).astype(BF16)
                both = _dot(qk, v_ext) + jnp.concatenate([s_inter, s_inter], axis=1) * _dot(
                    qc, cn_prev.astype(BF16))
                num, den = both[:, :ML_HD], both[:, ML_HD:]
                hacc[rows, hs] += num / jnp.maximum(jnp.abs(den), jnp.exp(-m_t))

                m_new = m_t[lr:lr + 1, :]
                b_last = b_t[lr:lr + 1, :]
                g_row = jnp.exp(b_last + r_row - m_new)
                decay = jnp.exp(b_last + m_prev - m_new)
                gkt = (ktc.astype(F32) * g_row).astype(BF16)
                cn_s[s] = jnp.concatenate([decay, decay], axis=1) * cn_prev + _dot(gkt, v_ext)
                m_s[s] = m_new

    if n_chunks <= 2:
        for c in range(n_chunks):
            step(c)
    else:
        def body(c, carry):
            step(c)
            return carry
        lax.fori_loop(0, n_chunks, body, 0)

    for c in range(n_chunks):
        rows = slice(c * CHUNK, (c + 1) * CHUNK)
        for h in range(ML_HEADS):
            hs = slice(h * ML_HD, (h + 1) * ML_HD)
            hn = _rms(hacc[rows, hs]) * ng_ref[0]
            y_ref[rows, hs] = (og_ref[rows, hs].astype(F32) * hn.astype(F32)).astype(BF16)

    if emit_state:
        for s in range(n_streams):
            cn = cn_s[s]
            co_ref[0, 0, s] = cn[:, :ML_HD]
            no_ref[0, 0, s:s + 1, :] = cn[:, ML_HD:].T[0:1, :]
            mo_ref[0, 0, s:s + 1, :] = m_s[s]


def _mlstm(z, kt, gates, gate_b, norm_g, init, carried, *, layer, n_layers, tok0, n_seq, seq_len, emit_state):
    n_tok = z.shape[0]
    s0 = tok0 // seq_len
    n_chunks = seq_len // CHUNK
    has_init = init is not None
    n_streams = 2 * ML_HEADS
    zblk = lambda zj: pl.BlockSpec((seq_len, BR_W), lambda b: (s0 + b, zj))
    in_specs = [zblk(ZJ_MQ), pl.BlockSpec((BR_W, seq_len), lambda b: (0, s0 + b)), zblk(ZJ_MV), zblk(ZJ_MO),
                pl.BlockSpec((seq_len, 128), lambda b: (s0 + b, 0)),
                pl.BlockSpec((1, 1, 128), lambda b: (layer, 0, 0)),
                pl.BlockSpec((1, 1, 128), lambda b: (layer, 0, 0))]
    args = [z, kt, z, z, gates, gate_b, norm_g]
    if has_init:
        c0, n0, m0 = init
        in_specs += [
            pl.BlockSpec((1, 1, 2, ML_HEADS, ML_HD, ML_HD), lambda b: (b, layer, 0, 0, 0, 0)),
            pl.BlockSpec((1, 1, 2, ML_HEADS, ML_HD), lambda b: (b, layer, 0, 0, 0)),
            pl.BlockSpec((1, n_streams, 128), lambda b: (b, 0, 0)),
        ]
        args += [c0, n0, m0]
    first_carried = len(args)
    in_specs += _carried(carried)
    args += list(carried)
    out_shape = [jax.ShapeDtypeStruct((n_tok, BR_W), BF16)]
    out_specs = [pl.BlockSpec((seq_len, BR_W), lambda b: (s0 + b, 0))]
    if emit_state:
        out_shape += [jax.ShapeDtypeStruct((n_seq, n_layers, n_streams, ML_HD, ML_HD), F32),
                      jax.ShapeDtypeStruct((n_seq, n_layers, n_streams, ML_HD), F32),
                      jax.ShapeDtypeStruct((n_seq, n_layers, n_streams, 128), F32)]
        out_specs += [pl.BlockSpec((1, 1, n_streams, ML_HD, ML_HD), lambda b: (b, layer, 0, 0, 0)),
                      pl.BlockSpec((1, 1, n_streams, ML_HD), lambda b: (b, layer, 0, 0)),
                      pl.BlockSpec((1, 1, n_streams, 128), lambda b: (b, layer, 0, 0))]
        aliases = {first_carried + n: 1 + n for n in range(len(carried))}
    else:
        aliases = {first_carried: 0}
    kern = functools.partial(_mlstm_kernel, n_chunks=n_chunks, has_init=has_init, emit_state=emit_state,
                             n_carried=len(carried))
    return pl.pallas_call(
        kern,
        out_shape=tuple(out_shape),
        grid=(n_seq,),
        in_specs=in_specs,
        out_specs=tuple(out_specs),
        input_output_aliases=aliases,
        scratch_shapes=[pltpu.VMEM((seq_len, BR_W), F32),
                        pltpu.VMEM((n_streams, ML_HD, 2 * ML_HD), F32),
                        pltpu.VMEM((n_streams, 1, 128), F32)],
        compiler_params=_cparams(("arbitrary",)),
        name="mlstm_init" if has_init else "mlstm_zero",
    )(*args)


def _merge_kernel(x_ref, mod_ref, yda_ref, yml_ref, sgu_ref, sgv_ref, g0_ref, g1_ref, g2_ref,
                  sgn_ref, sgw_ref, sgb_ref, wb_ref, wo_ref, o_ref):
    D = x_ref.shape[1]
    tm = x_ref.shape[0]
    v = sgv_ref[...].astype(F32)
    vc = v - jnp.mean(v, axis=-1, keepdims=True)
    sv = (vc * lax.rsqrt(jnp.mean(vc * vc, axis=-1, keepdims=True) + EPS) * sgn_ref[0]).astype(BF16)
    mixed = []
    for c in range(tm // CHUNK):
        rows = slice(c * CHUNK, (c + 1) * CHUNK)
        groups = [_dot(sgw_ref[0, g], sv[rows, g * 128:(g + 1) * 128]) for g in range(SG_GROUPS)]
        mixed.append(jnp.concatenate(groups, axis=1) + sgb_ref[0])
    y_sg = (sgu_ref[...].astype(F32) * jnp.concatenate(mixed, axis=0)).astype(BF16)

    m = g0_ref[...].astype(F32) * _dot(yda_ref[...], wb_ref[0, 0])
    m = m + g1_ref[...].astype(F32) * _dot(yml_ref[...], wb_ref[0, 1])
    m = m + g2_ref[...].astype(F32) * _dot(y_sg, wb_ref[0, 2])
    out = _dot(m.astype(BF16), wo_ref[0])
    o_ref[...] = x_ref[...] + mod_ref[0, :, 2 * D:3 * D] * out


def _mod_row_fn(layer, tile, n_ctx_tok, lat_len):
    def f(i):
        t0 = i * tile
        return layer * 8 + jnp.where(t0 < n_ctx_tok, 0, 1 + (t0 - n_ctx_tok) // lat_len)
    return f


def _merge(x, mod, y_da, y_ml, z, sg_norm_g, sg_w, sg_bias, w_branch, w_out, *, layer, n_ctx_tok, lat_len):
    n_tok, D = x.shape
    tm = MERGE_TILE
    mrow = _mod_row_fn(layer, tm, n_ctx_tok, lat_len)
    gate_blk = lambda n: pl.BlockSpec((tm, D), lambda i: (i, ZJ_GATE0 * COL_TILE // D + n))
    return pl.pallas_call(
        _merge_kernel,
        out_shape=jax.ShapeDtypeStruct((n_tok, D), F32),
        grid=(n_tok // tm,),
        in_specs=[
            pl.BlockSpec((tm, D), lambda i: (i, 0)),
            pl.BlockSpec((1, 1, 6 * D), lambda i: (mrow(i), 0, 0)),
            pl.BlockSpec((tm, BR_W), lambda i: (i, 0)),
            pl.BlockSpec((tm, BR_W), lambda i: (i, 0)),
            pl.BlockSpec((tm, COL_TILE), lambda i: (i, ZJ_SGU)),
            pl.BlockSpec((tm, COL_TILE), lambda i: (i, ZJ_SGV)),
            gate_blk(0), gate_blk(1), gate_blk(2),
            pl.BlockSpec((1, 1, BR_W), lambda i: (layer, 0, 0)),
            pl.BlockSpec((1, SG_GROUPS, CHUNK, CHUNK), lambda i: (layer, 0, 0, 0)),
            pl.BlockSpec((1, CHUNK, BR_W), lambda i: (layer, 0, 0)),
            pl.BlockSpec((1, N_BRANCH, BR_W, D), lambda i: (layer, 0, 0, 0)),
            pl.BlockSpec((1, D, D), lambda i: (layer, 0, 0)),
        ],
        out_specs=pl.BlockSpec((tm, D), lambda i: (i, 0)),
        compiler_params=_cparams(("arbitrary",)),
        name="merge",
    )(x, mod, y_da, y_ml, z, z, z, z, z, sg_norm_g, sg_w, sg_bias, w_branch, w_out)


def _ffn_kernel(x_ref, mod_ref, wa_ref, wg_ref, cwa_ref, cwg_ref, cba_ref, cbg_ref, wd_ref, fg_ref,
                *rest, n_ctx_tiles, ctx_len, n_chunks, final_norm):
    out_refs, (h_scr, acc, u0, u1) = rest[:-4], rest[-4:]
    ubufs = (u0, u1)
    i = pl.program_id(0)
    c = pl.program_id(1)
    D = x_ref.shape[1]
    is_lat = i >= n_ctx_tiles

    def run(par, up, down):
        if up:
            h = h_scr[...]
            ubufs[par][0] = _dot(h, wa_ref[0].astype(BF16))
            ubufs[par][1] = _dot(h, wg_ref[0].astype(BF16))
        if down:
            ua = _seq_conv3(ubufs[1 - par][0], cwa_ref, cba_ref, is_lat, ctx_len)
            ug = _seq_conv3(ubufs[1 - par][1], cwg_ref, cbg_ref, is_lat, ctx_len)
            a = (jax.nn.silu(ua) * ug).astype(BF16)
            acc[...] += _dot(a, wd_ref[0].astype(BF16))

    @pl.when(c == 0)
    def _():
        h = _rms(x_ref[...]) * (1.0 + mod_ref[0, :, 4 * D:5 * D]) + mod_ref[0, :, 3 * D:4 * D]
        h_scr[...] = h.astype(BF16)
        acc[...] = jnp.zeros_like(acc)
        run(0, True, False)

    steady = (c >= 1) & (c < n_chunks)
    for par in range(2):
        @pl.when(steady & (c % 2 == par))
        def _():
            run(par, True, True)

    last = c == n_chunks

    @pl.when(last)
    def _():
        run(n_chunks % 2, False, True)

    if final_norm:
        yp_ref, ys_ref = out_refs

        def result():
            return _rms(x_ref[...] + mod_ref[0, :, 5 * D:6 * D] * acc[...]) * fg_ref[...]

        @pl.when(last & jnp.logical_not(is_lat))
        def _():
            yp_ref[...] = result()

        @pl.when(last & is_lat)
        def _():
            ys_ref[...] = result()
    else:
        @pl.when(last)
        def _():
            out_refs[0][...] = x_ref[...] + mod_ref[0, :, 5 * D:6 * D] * acc[...]


def _ffn(x, mod, w_up, conv_w, conv_b, w_down, final_g, *, layer, n_ctx_tok, ctx_len, lat_len, final_norm):
    n_tok, D = x.shape
    d_ff = w_down.shape[1]
    tm, tc = TOKEN_TILE, FF_TILE
    nc = d_ff // tc
    n_ctx_tiles = n_ctx_tok // tm
    mrow = _mod_row_fn(layer, tm, n_ctx_tok, lat_len)
    assert nc >= 2 and lat_len == tm
    kern = functools.partial(_ffn_kernel, n_ctx_tiles=n_ctx_tiles, ctx_len=ctx_len, n_chunks=nc,
                             final_norm=final_norm)
    up_c = lambda c: jnp.minimum(c, nc - 1)
    down_c = lambda c: jnp.maximum(c - 1, 0)
    if final_norm:
        out_shape = (jax.ShapeDtypeStruct((n_ctx_tok, D), F32), jax.ShapeDtypeStruct((n_tok - n_ctx_tok, D), F32))
        out_specs = (pl.BlockSpec((tm, D), lambda i, c: (jnp.minimum(i, n_ctx_tiles - 1), 0)),
                     pl.BlockSpec((tm, D), lambda i, c: (jnp.maximum(i - n_ctx_tiles, 0), 0)))
    else:
        out_shape = jax.ShapeDtypeStruct((n_tok, D), F32)
        out_specs = pl.BlockSpec((tm, D), lambda i, c: (i, 0))
    return pl.pallas_call(
        kern,
        out_shape=out_shape,
        grid=(n_tok // tm, nc + 1),
        in_specs=[
            pl.BlockSpec((tm, D), lambda i, c: (i, 0)),
            pl.BlockSpec((1, 1, 6 * D), lambda i, c: (mrow(i), 0, 0)),
            pl.BlockSpec((1, D, tc), lambda i, c: (layer, 0, up_c(c))),
            pl.BlockSpec((1, D, tc), lambda i, c: (layer, 0, nc + up_c(c))),
            pl.BlockSpec((1, 3, tc), lambda i, c: (layer, 0, down_c(c))),
            pl.BlockSpec((1, 3, tc), lambda i, c: (layer, 0, nc + down_c(c))),
            pl.BlockSpec((1, 1, tc), lambda i, c: (layer, 0, down_c(c))),
            pl.BlockSpec((1, 1, tc), lambda i, c: (layer, 0, nc + down_c(c))),
            pl.BlockSpec((1, tc, D), lambda i, c: (layer, down_c(c), 0)),
            pl.BlockSpec((1, D), lambda i, c: (0, 0)),
        ],
        out_specs=out_specs,
        scratch_shapes=[pltpu.VMEM((tm, D), BF16), pltpu.VMEM((tm, D), F32),
                        pltpu.VMEM((2, tm, tc), F32), pltpu.VMEM((2, tm, tc), F32)],
        compiler_params=_cparams(("arbitrary", "arbitrary")),
        name="ffn_final" if final_norm else "ffn",
    )(x, mod, w_up, w_up, conv_w, conv_w, conv_b, conv_b, w_down, final_g)


def _rope_tables(n_pos):
    t = jnp.arange(n_pos)
    nf = DA_HD // 4
    inv = ROPE_THETA ** (-jnp.arange(nf, dtype=F32) / nf)
    ang = [(t // GRID_W).astype(F32)[:, None] * inv, (t % GRID_W).astype(F32)[:, None] * inv]
    zeros = jnp.zeros((n_pos, nf), F32)

    def lanes(first, second):
        sub = jnp.concatenate([first(ang[0]), second(ang[0]), first(ang[1]), second(ang[1])], axis=1)
        return jnp.concatenate([sub, sub], axis=1)

    cos = lanes(jnp.cos, jnp.cos)
    sin_a = lanes(lambda a: -jnp.sin(a), lambda a: zeros)
    sin_b = lanes(lambda a: zeros, jnp.sin)
    return cos, sin_a, sin_b


def kernel(x_prompt, x_sample, c, cache_k, cache_v, state_C, state_n, state_m, c_ctx, w_mod, b_mod, w_in,
           da_lambda, da_norm_g, ml_conv_w, ml_conv_b, ml_gate_b, ml_norm_g, sg_norm_g, sg_w, sg_b,
           w_branch, w_out, w_up, ffn_conv_w, ffn_conv_b, w_down, final_g):
    B, S, D = x_prompt.shape
    Bd, Sd, _ = x_sample.shape
    L = w_mod.shape[0]
    n_ctx_tok = B * S
    n_streams = 2 * ML_HEADS

    x = jnp.concatenate([x_prompt.reshape(n_ctx_tok, D), x_sample.reshape(Bd * Sd, D)], axis=0)
    cond8 = jnp.concatenate([c_ctx[None, :], c, jnp.zeros((8 - 1 - Bd, D), F32)], axis=0)
    mod = _modulation(cond8, w_mod, b_mod).reshape(L * 8, 1, 6 * D)
    rope_tabs = _rope_tables(Sd)

    da_w = DA_HEADS * 2 * DA_HD
    o_mlg = 3 * da_w + 4 * BR_W
    o_sg = o_mlg + 4 * ML_HEADS
    o_gate = o_sg + 2 * BR_W
    wz = jnp.concatenate([w_in[:, :, :o_mlg], w_in[:, :, o_sg:o_sg + BR_W], w_in[:, :, o_gate:],
                          w_in[:, :, o_sg + BR_W:o_gate]], axis=2).astype(BF16)
    lane_pad = 128 - 4 * ML_HEADS
    wg = jnp.pad(w_in[:, :, o_mlg:o_sg], ((0, 0), (0, 0), (0, lane_pad))).astype(BF16)
    gate_b = jnp.pad(ml_gate_b.reshape(L, 1, -1), ((0, 0), (0, 0), (0, lane_pad)))
    sg_bias = jnp.repeat(jnp.swapaxes(sg_b, 1, 2), BR_W // SG_GROUPS, axis=2)
    sg_w16 = sg_w.astype(BF16)
    w_branch16 = w_branch.astype(BF16)
    w_out16 = w_out.astype(BF16)
    row = lambda p: p.reshape(L, 1, -1)

    caches, states = (), ()
    for l in range(L):
        lam_init = 0.8 - 0.6 * math.exp(-0.3 * l)
        z, gates, kt = _inproj(x, mod, wz, wg, ml_conv_w, row(ml_conv_b), rope_tabs, layer=l,
                               n_ctx_tok=n_ctx_tok, ctx_len=S, lat_len=Sd)

        y_da, *caches = _attn_ctx(z, da_lambda, row(da_norm_g), tuple(caches), layer=l, n_layers=L,
                                  n_seq=B, seq_len=S, lam_init=lam_init)
        y_da = _attn_lat(z, cache_k, cache_v, da_lambda, row(da_norm_g), y_da, layer=l,
                         n_ctx_tok=n_ctx_tok, n_seq=Bd, seq_len=Sd, lam_init=lam_init)

        y_ml, *states = _mlstm(z, kt, gates, gate_b, row(ml_norm_g), None, tuple(states), layer=l,
                               n_layers=L, tok0=0, n_seq=B, seq_len=S, emit_state=True)
        m0 = jnp.broadcast_to(state_m[:, l].reshape(Bd, n_streams, 1), (Bd, n_streams, 128))
        (y_ml,) = _mlstm(z, kt, gates, gate_b, row(ml_norm_g), (state_C, state_n, m0), (y_ml,), layer=l,
                         n_layers=L, tok0=n_ctx_tok, n_seq=Bd, seq_len=Sd, emit_state=False)

        x = _merge(x, mod, y_da, y_ml, z, row(sg_norm_g), sg_w16, sg_bias, w_branch16, w_out16,
                   layer=l, n_ctx_tok=n_ctx_tok, lat_len=Sd)
        x = _ffn(x, mod, w_up, ffn_conv_w, row(ffn_conv_b), w_down, final_g.reshape(1, -1), layer=l,
                 n_ctx_tok=n_ctx_tok, ctx_len=S, lat_len=Sd, final_norm=(l == L - 1))

    y_prompt, y_sample = x
    new_k, new_v = caches
    new_C, new_n, new_m = states
    return (y_prompt.reshape(B, S, D), y_sample.reshape(Bd, Sd, D), new_k, new_v,
            new_C.reshape(B, L, 2, ML_HEADS, ML_HD, ML_HD), new_n.reshape(B, L, 2, ML_HEADS, ML_HD),
            new_m[:, :, :, 0].reshape(B, L, 2, ML_HEADS))
```

```python
import functools
import math

import jax
import jax.numpy as jnp
from jax import lax
from jax.experimental import pallas as pl
from jax.experimental.pallas import tpu as pltpu

F32 = jnp.float32
BF16 = jnp.bfloat16

GRID_W = 64
DA_HEADS = 4
DA_HD = 64
ML_HEADS = 4
ML_HD = 128
CHUNK = 128
SG_GROUPS = 4
BR_W = 512
N_BRANCH = 3
ROPE_THETA = 10000.0
EPS = 1e-6
NEG = -1e30

VMEM_LIMIT_BYTES = 52 * 1024 * 1024

TOKEN_TILE = 1024
COL_TILE = 512
MERGE_TILE = 512
FF_TILE = 256
Q_TILE = 256

ZJ_Q, ZJ_K, ZJ_V, ZJ_MQ, ZJ_MK, ZJ_MV, ZJ_MO, ZJ_SGU = range(8)
ZJ_GATE0 = 8
ZJ_SGV = 14
N_ZJ = 15


def _cparams(sem):
    return pltpu.CompilerParams(dimension_semantics=sem, vmem_limit_bytes=VMEM_LIMIT_BYTES)


def _dot(a, b):
    return jnp.dot(a, b, preferred_element_type=F32)


def _dot_nt(a, b):
    return lax.dot_general(a, b, (((1,), (1,)), ((), ())), preferred_element_type=F32)


def _dot_tn(a, b):
    return lax.dot_general(a, b, (((0,), (0,)), ((), ())), preferred_element_type=F32)


def _mod_kernel(cond_ref, w_ref, b_ref, o_ref):
    a = jax.nn.silu(cond_ref[...]).astype(BF16)
    o_ref[0] = _dot(a, w_ref[0].astype(BF16)) + b_ref[0]


def _modulation(cond8, w_mod, b_mod):
    L, D, N = w_mod.shape
    tn = 768
    return pl.pallas_call(
        _mod_kernel,
        out_shape=jax.ShapeDtypeStruct((L, 8, N), F32),
        grid=(L, N // tn),
        in_specs=[
            pl.BlockSpec((8, D), lambda l, j: (0, 0)),
            pl.BlockSpec((1, D, tn), lambda l, j: (l, 0, j)),
            pl.BlockSpec((1, 1, tn), lambda l, j: (l, 0, j)),
        ],
        out_specs=pl.BlockSpec((1, 8, tn), lambda l, j: (l, 0, j)),
        compiler_params=_cparams(("arbitrary", "arbitrary")),
        name="modulation",
    )(cond8, w_mod, b_mod.reshape(L, 1, N))


N_GATE_COLS = 4 * ML_HEADS
SRC_MLG_TILE = 7


def _wprep_kernel(a_ref, b_ref, wz_ref, wg_ref):
    t = pl.program_id(1)

    @pl.when(t < SRC_MLG_TILE)
    def _():
        wz_ref[0] = a_ref[0].astype(BF16)

    @pl.when(t >= SRC_MLG_TILE)
    def _():
        both = jnp.concatenate([a_ref[0], b_ref[0]], axis=1)
        width = both.shape[1]
        wz_ref[0] = pltpu.roll(both, width - N_GATE_COLS, axis=1)[:, :COL_TILE].astype(BF16)

    @pl.when(t == ZJ_SGU)
    def _():
        lane = lax.broadcasted_iota(jnp.int32, (1, 128), 1)
        wg_ref[0] = jnp.where(lane < N_GATE_COLS, a_ref[0, :, 0:128], 0.0).astype(BF16)


def _prep_w_in(w_in):
    L, D, n_in = w_in.shape
    assert n_in == N_ZJ * COL_TILE + N_GATE_COLS

    def src_tile(t):
        return jnp.where(t <= ZJ_SGU, t, jnp.where(t == ZJ_SGV, ZJ_SGU + 1, t + 1))

    lanes_per_tile = COL_TILE // 128
    return pl.pallas_call(
        _wprep_kernel,
        out_shape=(jax.ShapeDtypeStruct((L, D, N_ZJ * COL_TILE), BF16),
                   jax.ShapeDtypeStruct((L, D, 128), BF16)),
        grid=(L, N_ZJ),
        in_specs=[pl.BlockSpec((1, D, COL_TILE), lambda l, t: (l, 0, src_tile(t))),
                  pl.BlockSpec((1, D, 128), lambda l, t: (l, 0, (src_tile(t) + 1) * lanes_per_tile))],
        out_specs=(pl.BlockSpec((1, D, COL_TILE), lambda l, t: (l, 0, t)),
                   pl.BlockSpec((1, D, 128), lambda l, t: (l, 0, 0))),
        compiler_params=_cparams(("arbitrary", "arbitrary")),
        name="w_in_prep",
    )(w_in, w_in)


def _rms(x):
    return x * lax.rsqrt(jnp.mean(x * x, axis=-1, keepdims=True) + EPS)


def _sigmoid(x):
    return 0.5 * jnp.tanh(0.5 * x) + 0.5


def _silu(x):
    return x * _sigmoid(x)


def _seq_conv3(u, w_ref, b_ref, is_lat, ctx_len):
    rows = u.shape[0]
    prev = pltpu.roll(u, 1, axis=0)
    nxt = pltpu.roll(u, rows - 1, axis=0)
    r8 = lax.broadcasted_iota(jnp.int32, (8, 1), 0)
    interior = jnp.where(is_lat, 1.0, 0.0).astype(F32)
    n_pieces = rows // ctx_len
    pp, nn = [], []
    for p in range(n_pieces):
        lo, hi = p * ctx_len, (p + 1) * ctx_len
        keep_first = interior if p > 0 else 0.0
        keep_last = interior if p < n_pieces - 1 else 0.0
        pp += [prev[lo:lo + 8] * jnp.where(r8 == 0, keep_first, 1.0), prev[lo + 8:hi]]
        nn += [nxt[lo:hi - 8], nxt[hi - 8:hi] * jnp.where(r8 == 7, keep_last, 1.0)]
    prev = jnp.concatenate(pp, axis=0)
    nxt = jnp.concatenate(nn, axis=0)
    return prev * w_ref[0, 0:1, :] + u * w_ref[0, 1:2, :] + nxt * w_ref[0, 2:3, :] + b_ref[0]


def _inproj_kernel(x_ref, mod_ref, wz_ref, wg_ref, cw_ref, cb_ref, cos_ref, sa_ref, sb_ref,
                   z_ref, g_ref, kt_ref, h_scr, *, n_ctx_tiles, ctx_len, lat_len):
    i = pl.program_id(0)
    j = pl.program_id(1)
    D = x_ref.shape[1]

    @pl.when(j == 0)
    def _():
        h = _rms(x_ref[...]) * (1.0 + mod_ref[0, :, D:2 * D]) + mod_ref[0, :, 0:D]
        hb = h.astype(BF16)
        h_scr[...] = hb
        g_ref[...] = _dot(hb, wg_ref[0])

    z = _dot(h_scr[...], wz_ref[0])
    is_lat = i >= n_ctx_tiles

    def rope(zz):
        parts = []
        for hd in range(zz.shape[1] // 128):
            xh = zz[:, hd * 128:(hd + 1) * 128]
            parts.append(xh * cos_ref[...] + pltpu.roll(xh, 112, axis=1) * sa_ref[...]
                         + pltpu.roll(xh, 16, axis=1) * sb_ref[...])
        return jnp.concatenate(parts, axis=1)

    @pl.when((j == ZJ_Q) & is_lat)
    def _():
        z_ref[...] = (rope(z) * (DA_HD ** -0.5)).astype(BF16)

    @pl.when((j == ZJ_Q) & jnp.logical_not(is_lat))
    def _():
        z_ref[...] = (z * (DA_HD ** -0.5)).astype(BF16)

    @pl.when((j == ZJ_K) & is_lat)
    def _():
        z_ref[...] = rope(z).astype(BF16)

    @pl.when(((j == ZJ_K) & jnp.logical_not(is_lat)) | (j == ZJ_V) | (j == ZJ_MV))
    def _():
        z_ref[...] = z.astype(BF16)

    @pl.when(j == ZJ_MQ)
    def _():
        z_ref[...] = _silu(_seq_conv3(z, cw_ref, cb_ref, is_lat, ctx_len)).astype(BF16)

    @pl.when(j == ZJ_MK)
    def _():
        a = _silu(_seq_conv3(z, cw_ref, cb_ref, is_lat, ctx_len)) * (ML_HD ** -0.5)
        kt_ref[...] = a.T.astype(BF16)

    @pl.when((j == ZJ_MO) | ((j >= ZJ_GATE0) & (j < ZJ_SGV)))
    def _():
        z_ref[...] = _sigmoid(z).astype(BF16)

    @pl.when((j == ZJ_SGU) | (j == ZJ_SGV))
    def _():
        z_ref[...] = jax.nn.gelu(z, approximate=True).astype(BF16)


def _inproj(x, mod, wz, wg, conv_w, conv_b, rope_tabs, *, layer, n_ctx_tok, ctx_len, lat_len):
    n_tok, D = x.shape
    tm, tn = TOKEN_TILE, COL_TILE
    n_ctx_tiles = n_ctx_tok // tm
    assert lat_len == tm and tm % ctx_len == 0 and n_ctx_tok % tm == 0

    def mod_row(i):
        return layer * 8 + jnp.where(i < n_ctx_tiles, 0, i - n_ctx_tiles + 1)

    def conv_col(j):
        return jnp.clip(j - ZJ_MQ, 0, 1)

    kern = functools.partial(_inproj_kernel, n_ctx_tiles=n_ctx_tiles, ctx_len=ctx_len, lat_len=lat_len)
    return pl.pallas_call(
        kern,
        out_shape=(jax.ShapeDtypeStruct((n_tok, N_ZJ * tn), BF16),
                   jax.ShapeDtypeStruct((n_tok, 128), F32),
                   jax.ShapeDtypeStruct((tn, n_tok), BF16)),
        grid=(n_tok // tm, N_ZJ),
        in_specs=[
            pl.BlockSpec((tm, D), lambda i, j: (i, 0)),
            pl.BlockSpec((1, 1, 2 * D), lambda i, j: (mod_row(i), 0, 0)),
            pl.BlockSpec((1, D, tn), lambda i, j: (layer, 0, j)),
            pl.BlockSpec((1, D, 128), lambda i, j: (layer, 0, 0)),
            pl.BlockSpec((1, 3, tn), lambda i, j: (layer, 0, conv_col(j))),
            pl.BlockSpec((1, 1, tn), lambda i, j: (layer, 0, conv_col(j))),
            pl.BlockSpec((tm, 128), lambda i, j: (0, 0)),
            pl.BlockSpec((tm, 128), lambda i, j: (0, 0)),
            pl.BlockSpec((tm, 128), lambda i, j: (0, 0)),
        ],
        out_specs=(pl.BlockSpec((tm, tn), lambda i, j: (i, jnp.where(j == ZJ_MK, ZJ_MQ, j))),
                   pl.BlockSpec((tm, 128), lambda i, j: (i, 0)),
                   pl.BlockSpec((tn, tm), lambda i, j: (0, i))),
        scratch_shapes=[pltpu.VMEM((tm, D), BF16)],
        compiler_params=_cparams(("arbitrary", "arbitrary")),
        name="inproj",
    )(x, mod, wz, wg, conv_w, conv_b, *rope_tabs)


def _diff_attn_core(q, ks, vs, lam, gain):
    lane = lax.broadcasted_iota(jnp.int32, (1, 128), 1)
    lo = lane < DA_HD
    zero = jnp.zeros_like(q)
    q1 = jnp.where(lo, q, zero)
    q2 = jnp.where(lo, zero, q)

    def softmax_times_v(qh):
        s = [_dot_nt(qh, k) for k in ks]
        mx = functools.reduce(jnp.maximum, [jnp.max(t, axis=-1, keepdims=True) for t in s])
        e = [jnp.exp(t - mx) for t in s]
        den = functools.reduce(jnp.add, [jnp.sum(t, axis=-1, keepdims=True) for t in e])
        o = functools.reduce(jnp.add, [_dot(t.astype(BF16), v) for t, v in zip(e, vs)])
        return o, 1.0 / den

    o1, r1 = softmax_times_v(q1)
    o2, r2 = softmax_times_v(q2)
    return _rms(o1 * r1 - o2 * (lam * r2)) * gain


def _lambda_full(lam_ref, lam_init):
    p = lam_ref[0]
    s1 = jnp.sum(p[0:1, :] * p[1:2, :], axis=-1, keepdims=True)
    s2 = jnp.sum(p[2:3, :] * p[3:4, :], axis=-1, keepdims=True)
    return jnp.exp(s1) - jnp.exp(s2) + lam_init


def _carried(arrays):
    return [pl.BlockSpec(memory_space=pl.ANY)] * len(arrays)


def _attn_ctx_kernel(*refs, lam_init, n_carried):
    q_ref, k_ref, v_ref, lam_ref, g_ref = refs[:5]
    y_ref, ck_ref, cv_ref = refs[5 + n_carried:]
    lam = _lambda_full(lam_ref, lam_init)
    for h in range(DA_HEADS):
        hs = slice(h * 128, (h + 1) * 128)
        k = k_ref[:, hs]
        v = v_ref[:, hs]
        o = _diff_attn_core(q_ref[:, hs], [k], [v], lam, g_ref[0])
        y_ref[:, hs] = (o * (1.0 - lam_init)).astype(BF16)
        ck_ref[0, 0, h] = k.astype(F32)
        cv_ref[0, 0, h] = v.astype(F32)


def _attn_ctx(z, da_lam, norm_g, carried, *, layer, n_layers, n_seq, seq_len, lam_init):
    n_tok = z.shape[0]
    kern = functools.partial(_attn_ctx_kernel, lam_init=lam_init, n_carried=len(carried))
    blk = lambda zj: pl.BlockSpec((seq_len, COL_TILE), lambda b: (b, zj))
    cache_sds = jax.ShapeDtypeStruct((n_seq, n_layers, DA_HEADS, seq_len, 128), F32)
    cache_blk = pl.BlockSpec((1, 1, DA_HEADS, seq_len, 128), lambda b: (b, layer, 0, 0, 0))
    return pl.pallas_call(
        kern,
        out_shape=(jax.ShapeDtypeStruct((n_tok, BR_W), BF16), cache_sds, cache_sds),
        grid=(n_seq,),
        in_specs=[blk(ZJ_Q), blk(ZJ_K), blk(ZJ_V),
                  pl.BlockSpec((1, 4, DA_HD), lambda b: (layer, 0, 0)),
                  pl.BlockSpec((1, 1, 128), lambda b: (layer, 0, 0))] + _carried(carried),
        out_specs=(pl.BlockSpec((seq_len, BR_W), lambda b: (b, 0)), cache_blk, cache_blk),
        input_output_aliases={5 + n: 1 + n for n in range(len(carried))},
        compiler_params=_cparams(("arbitrary",)),
        name="attn_ctx",
    )(z, z, z, da_lam, norm_g, *carried)


def _attn_lat_kernel(q_ref, k_ref, v_ref, ck_ref, cv_ref, lam_ref, g_ref, yin_ref, y_ref, *, lam_init):
    del yin_ref
    lam = _lambda_full(lam_ref, lam_init)
    ks = [ck_ref[0, 0, 0].astype(BF16), k_ref[...]]
    vs = [cv_ref[0, 0, 0].astype(BF16), v_ref[...]]
    for t in range(q_ref.shape[0] // Q_TILE):
        rows = slice(t * Q_TILE, (t + 1) * Q_TILE)
        o = _diff_attn_core(q_ref[rows, :], ks, vs, lam, g_ref[0])
        y_ref[rows, :] = (o * (1.0 - lam_init)).astype(BF16)


def _attn_lat(z, cache_k, cache_v, da_lam, norm_g, y_da, *, layer, n_ctx_tok, n_seq, seq_len, lam_init):
    hb = COL_TILE // 128
    s0 = n_ctx_tok // seq_len
    past = cache_k.shape[3]
    kern = functools.partial(_attn_lat_kernel, lam_init=lam_init)
    zblk = lambda zj: pl.BlockSpec((seq_len, 128), lambda b, h: (s0 + b, zj * hb + h))
    return pl.pallas_call(
        kern,
        out_shape=jax.ShapeDtypeStruct(y_da.shape, y_da.dtype),
        grid=(n_seq, DA_HEADS),
        in_specs=[
            zblk(ZJ_Q), zblk(ZJ_K), zblk(ZJ_V),
            pl.BlockSpec((1, 1, 1, past, 128), lambda b, h: (b, layer, h, 0, 0)),
            pl.BlockSpec((1, 1, 1, past, 128), lambda b, h: (b, layer, h, 0, 0)),
            pl.BlockSpec((1, 4, DA_HD), lambda b, h: (layer, 0, 0)),
            pl.BlockSpec((1, 1, 128), lambda b, h: (layer, 0, 0)),
            pl.BlockSpec(memory_space=pl.ANY),
        ],
        out_specs=pl.BlockSpec((seq_len, 128), lambda b, h: (s0 + b, h)),
        input_output_aliases={7: 0},
        compiler_params=_cparams(("arbitrary", "arbitrary")),
        name="attn_lat",
    )(z, z, z, cache_k, cache_v, da_lam, norm_g, y_da)


def _split3(x):
    hi = x.astype(BF16)
    r = x - hi.astype(F32)
    mid = r.astype(BF16)
    lo = (r - mid.astype(F32)).astype(BF16)
    return hi, mid, lo


def _mlstm_kernel(*refs, n_chunks, has_init, emit_state, n_carried):
    q_ref, kt_ref, v_ref, og_ref, gt_ref, gb_ref, ng_ref = refs[:7]
    pos = 7
    if has_init:
        c0_ref, n0_ref, m0_ref = refs[pos:pos + 3]
        pos += 3
    pos += n_carried
    y_ref = refs[pos]
    pos += 1
    if emit_state:
        co_ref, no_ref, mo_ref = refs[pos:pos + 3]
        pos += 3
    hacc, cn_s, m_s = refs[pos:pos + 3]

    n_streams = 2 * ML_HEADS
    hacc[...] = jnp.zeros_like(hacc)
    for s in range(n_streams):
        d, h = divmod(s, ML_HEADS)
        if has_init:
            n_col = jnp.broadcast_to(n0_ref[0, 0, d, h:h + 1, :], (ML_HD, ML_HD)).T
            cn_s[s] = jnp.concatenate([c0_ref[0, 0, d, h], n_col], axis=1)
            m_s[s] = m0_ref[0, s:s + 1, :]
        else:
            cn_s[s] = jnp.zeros((ML_HD, 2 * ML_HD), F32)
            m_s[s] = jnp.zeros((1, 128), F32)

    row = lax.broadcasted_iota(jnp.int32, (CHUNK, CHUNK), 0)
    col = lax.broadcasted_iota(jnp.int32, (CHUNK, CHUNK), 1)
    masks = (col <= row, col >= row)
    tris = tuple(jnp.where(m, 1.0, 0.0).astype(BF16) for m in masks)
    last_row = (CHUNK - 1, 0)

    def step(c_fw):
        for d in range(2):
            c = c_fw if d == 0 else n_chunks - 1 - c_fw
            r0 = c * CHUNK
            if not isinstance(r0, int):
                r0 = pl.multiple_of(r0, CHUNK)
            rows = pl.ds(r0, CHUNK)
            pre = gt_ref[rows, :] + gb_ref[0]
            logf = jax.nn.log_sigmoid(pre)
            hi, mid, lo = _split3(logf)
            csum = _dot(tris[d], hi) + _dot(tris[d], mid) + _dot(tris[d], lo)
            pre_t = pre.T
            csum_t = csum.T
            lr = last_row[d]
            ones = jnp.ones((CHUNK, ML_HD), BF16)
            for h in range(ML_HEADS):
                s = d * ML_HEADS + h
                li, lf = d * ML_HEADS + h, 2 * ML_HEADS + d * ML_HEADS + h
                b_t = jnp.broadcast_to(csum[:, lf:lf + 1], (CHUNK, 128))
                r_row = pre_t[li:li + 1, :] - csum_t[lf:lf + 1, :]
                m_prev = m_s[s]
                cn_prev = cn_s[s]
                hs = slice(h * ML_HD, (h + 1) * ML_HD)
                qc = q_ref[rows, hs]
                ktc = kt_ref[hs, rows]
                v_ext = jnp.concatenate([v_ref[rows, hs], ones], axis=1)

                log_w = jnp.where(masks[d], b_t + r_row, NEG)
                inter = b_t + m_prev
                m_t = jnp.maximum(inter, jnp.broadcast_to(jnp.max(log_w, axis=-1, keepdims=True),
                                                          (CHUNK, 128)))
                w = jnp.exp(log_w - m_t)
                s_inter = jnp.exp(inter - m_t)
                qk = (_dot(qc, ktc) * w).astype(BF16)
                both = _dot(qk, v_ext) + jnp.concatenate([s_inter, s_inter], axis=1) * _dot(
                    qc, cn_prev.astype(BF16))
                num, den = both[:, :ML_HD], both[:, ML_HD:]
                hacc[rows, hs] += num / jnp.maximum(jnp.abs(den), jnp.exp(-m_t))

                m_new = m_t[lr:lr + 1, :]
                b_last = b_t[lr:lr + 1, :]
                g_row = jnp.exp(b_last + r_row - m_new)
                decay = jnp.exp(b_last + m_prev - m_new)
                gkt = (ktc.astype(F32) * g_row).astype(BF16)
                cn_s[s] = jnp.concatenate([decay, decay], axis=1) * cn_prev + _dot(gkt, v_ext)
                m_s[s] = m_new

    if n_chunks <= 2:
        for c in range(n_chunks):
            step(c)
    else:
        def body(c, carry):
            step(c)
            return carry
        lax.fori_loop(0, n_chunks, body, 0)

    for c in range(n_chunks):
        rows = slice(c * CHUNK, (c + 1) * CHUNK)
        for h in range(ML_HEADS):
            hs = slice(h * ML_HD, (h + 1) * ML_HD)
            hn = _rms(hacc[rows, hs]) * ng_ref[0]
            y_ref[rows, hs] = (og_ref[rows, hs].astype(F32) * hn.astype(F32)).astype(BF16)

    if emit_state:
        for s in range(n_streams):
            cn = cn_s[s]
            co_ref[0, 0, s] = cn[:, :ML_HD]
            no_ref[0, 0, s:s + 1, :] = cn[:, ML_HD:].T[0:1, :]
            mo_ref[0, 0, s:s + 1, :] = m_s[s]


def _mlstm(z, kt, gates, gate_b, norm_g, init, carried, *, layer, n_layers, tok0, n_seq, seq_len, emit_state):
    n_tok = z.shape[0]
    s0 = tok0 // seq_len
    n_chunks = seq_len // CHUNK
    has_init = init is not None
    n_streams = 2 * ML_HEADS
    zblk = lambda zj: pl.BlockSpec((seq_len, BR_W), lambda b: (s0 + b, zj))
    in_specs = [zblk(ZJ_MQ), pl.BlockSpec((BR_W, seq_len), lambda b: (0, s0 + b)), zblk(ZJ_MV), zblk(ZJ_MO),
                pl.BlockSpec((seq_len, 128), lambda b: (s0 + b, 0)),
                pl.BlockSpec((1, 1, 128), lambda b: (layer, 0, 0)),
                pl.BlockSpec((1, 1, 128), lambda b: (layer, 0, 0))]
    args = [z, kt, z, z, gates, gate_b, norm_g]
    if has_init:
        c0, n0, m0 = init
        in_specs += [
            pl.BlockSpec((1, 1, 2, ML_HEADS, ML_HD, ML_HD), lambda b: (b, layer, 0, 0, 0, 0)),
            pl.BlockSpec((1, 1, 2, ML_HEADS, ML_HD), lambda b: (b, layer, 0, 0, 0)),
            pl.BlockSpec((1, n_streams, 128), lambda b: (b, 0, 0)),
        ]
        args += [c0, n0, m0]
    first_carried = len(args)
    in_specs += _carried(carried)
    args += list(carried)
    out_shape = [jax.ShapeDtypeStruct((n_tok, BR_W), BF16)]
    out_specs = [pl.BlockSpec((seq_len, BR_W), lambda b: (s0 + b, 0))]
    if emit_state:
        out_shape += [jax.ShapeDtypeStruct((n_seq, n_layers, n_streams, ML_HD, ML_HD), F32),
                      jax.ShapeDtypeStruct((n_seq, n_layers, n_streams, ML_HD), F32),
                      jax.ShapeDtypeStruct((n_seq, n_layers, n_streams, 128), F32)]
        out_specs += [pl.BlockSpec((1, 1, n_streams, ML_HD, ML_HD), lambda b: (b, layer, 0, 0, 0)),
                      pl.BlockSpec((1, 1, n_streams, ML_HD), lambda b: (b, layer, 0, 0)),
                      pl.BlockSpec((1, 1, n_streams, 128), lambda b: (b, layer, 0, 0))]
        aliases = {first_carried + n: 1 + n for n in range(len(carried))}
    else:
        aliases = {first_carried: 0}
    kern = functools.partial(_mlstm_kernel, n_chunks=n_chunks, has_init=has_init, emit_state=emit_state,
                             n_carried=len(carried))
    return pl.pallas_call(
        kern,
        out_shape=tuple(out_shape),
        grid=(n_seq,),
        in_specs=in_specs,
        out_specs=tuple(out_specs),
        input_output_aliases=aliases,
        scratch_shapes=[pltpu.VMEM((seq_len, BR_W), F32),
                        pltpu.VMEM((n_streams, ML_HD, 2 * ML_HD), F32),
                        pltpu.VMEM((n_streams, 1, 128), F32)],
        compiler_params=_cparams(("arbitrary",)),
        name="mlstm_init" if has_init else "mlstm_zero",
    )(*args)


def _merge_kernel(x_ref, mod_ref, yda_ref, yml_ref, sgu_ref, sgv_ref, g0_ref, g1_ref, g2_ref,
                  sgn_ref, sgw_ref, sgb_ref, wb_ref, wo_ref, o_ref):
    D = x_ref.shape[1]
    tm = x_ref.shape[0]
    v = sgv_ref[...].astype(F32)
    vc = v - jnp.mean(v, axis=-1, keepdims=True)
    sv = (vc * lax.rsqrt(jnp.mean(vc * vc, axis=-1, keepdims=True) + EPS) * sgn_ref[0]).astype(BF16)
    mixed = []
    for c in range(tm // CHUNK):
        rows = slice(c * CHUNK, (c + 1) * CHUNK)
        groups = [_dot(sgw_ref[0, g], sv[rows, g * 128:(g + 1) * 128]) for g in range(SG_GROUPS)]
        mixed.append(jnp.concatenate(groups, axis=1) + sgb_ref[0])
    y_sg = (sgu_ref[...].astype(F32) * jnp.concatenate(mixed, axis=0)).astype(BF16)

    m = g0_ref[...].astype(F32) * _dot(yda_ref[...], wb_ref[0, 0])
    m = m + g1_ref[...].astype(F32) * _dot(yml_ref[...], wb_ref[0, 1])
    m = m + g2_ref[...].astype(F32) * _dot(y_sg, wb_ref[0, 2])
    out = _dot(m.astype(BF16), wo_ref[0])
    o_ref[...] = x_ref[...] + mod_ref[0, :, 2 * D:3 * D] * out


def _mod_row_fn(layer, tile, n_ctx_tok, lat_len):
    def f(i):
        t0 = i * tile
        return layer * 8 + jnp.where(t0 < n_ctx_tok, 0, 1 + (t0 - n_ctx_tok) // lat_len)
    return f


def _merge(x, mod, y_da, y_ml, z, sg_norm_g, sg_w, sg_bias, w_branch, w_out, *, layer, n_ctx_tok, lat_len):
    n_tok, D = x.shape
    tm = MERGE_TILE
    mrow = _mod_row_fn(layer, tm, n_ctx_tok, lat_len)
    gate_blk = lambda n: pl.BlockSpec((tm, D), lambda i: (i, ZJ_GATE0 * COL_TILE // D + n))
    return pl.pallas_call(
        _merge_kernel,
        out_shape=jax.ShapeDtypeStruct((n_tok, D), F32),
        grid=(n_tok // tm,),
        in_specs=[
            pl.BlockSpec((tm, D), lambda i: (i, 0)),
            pl.BlockSpec((1, 1, 6 * D), lambda i: (mrow(i), 0, 0)),
            pl.BlockSpec((tm, BR_W), lambda i: (i, 0)),
            pl.BlockSpec((tm, BR_W), lambda i: (i, 0)),
            pl.BlockSpec((tm, COL_TILE), lambda i: (i, ZJ_SGU)),
            pl.BlockSpec((tm, COL_TILE), lambda i: (i, ZJ_SGV)),
            gate_blk(0), gate_blk(1), gate_blk(2),
            pl.BlockSpec((1, 1, BR_W), lambda i: (layer, 0, 0)),
            pl.BlockSpec((1, SG_GROUPS, CHUNK, CHUNK), lambda i: (layer, 0, 0, 0)),
            pl.BlockSpec((1, CHUNK, BR_W), lambda i: (layer, 0, 0)),
            pl.BlockSpec((1, N_BRANCH, BR_W, D), lambda i: (layer, 0, 0, 0)),
            pl.BlockSpec((1, D, D), lambda i: (layer, 0, 0)),
        ],
        out_specs=pl.BlockSpec((tm, D), lambda i: (i, 0)),
        compiler_params=_cparams(("arbitrary",)),
        name="merge",
    )(x, mod, y_da, y_ml, z, z, z, z, z, sg_norm_g, sg_w, sg_bias, w_branch, w_out)


def _ffn_kernel(x_ref, mod_ref, wa_ref, wg_ref, cwa_ref, cwg_ref, cba_ref, cbg_ref, wd_ref, fg_ref,
                *rest, n_ctx_tiles, ctx_len, n_chunks, final_norm):
    out_refs, (h_scr, acc, u0, u1) = rest[:-4], rest[-4:]
    ubufs = (u0, u1)
    i = pl.program_id(0)
    c = pl.program_id(1)
    D = x_ref.shape[1]
    is_lat = i >= n_ctx_tiles

    def run(par, up, down):
        if up:
            h = h_scr[...]
            ubufs[par][0] = _dot(h, wa_ref[0].astype(BF16))
            ubufs[par][1] = _dot(h, wg_ref[0].astype(BF16))
        if down:
            ua = _seq_conv3(ubufs[1 - par][0], cwa_ref, cba_ref, is_lat, ctx_len)
            ug = _seq_conv3(ubufs[1 - par][1], cwg_ref, cbg_ref, is_lat, ctx_len)
            a = (_silu(ua) * ug).astype(BF16)
            acc[...] += _dot(a, wd_ref[0].astype(BF16))

    @pl.when(c == 0)
    def _():
        h = _rms(x_ref[...]) * (1.0 + mod_ref[0, :, 4 * D:5 * D]) + mod_ref[0, :, 3 * D:4 * D]
        h_scr[...] = h.astype(BF16)
        acc[...] = jnp.zeros_like(acc)
        run(0, True, False)

    steady = (c >= 1) & (c < n_chunks)
    for par in range(2):
        @pl.when(steady & (c % 2 == par))
        def _():
            run(par, True, True)

    last = c == n_chunks

    @pl.when(last)
    def _():
        run(n_chunks % 2, False, True)

    if final_norm:
        yp_ref, ys_ref = out_refs

        def result():
            return _rms(x_ref[...] + mod_ref[0, :, 5 * D:6 * D] * acc[...]) * fg_ref[...]

        @pl.when(last & jnp.logical_not(is_lat))
        def _():
            yp_ref[...] = result()

        @pl.when(last & is_lat)
        def _():
            ys_ref[...] = result()
    else:
        @pl.when(last)
        def _():
            out_refs[0][...] = x_ref[...] + mod_ref[0, :, 5 * D:6 * D] * acc[...]


def _ffn(x, mod, w_up, conv_w, conv_b, w_down, final_g, *, layer, n_ctx_tok, ctx_len, lat_len, final_norm):
    n_tok, D = x.shape
    d_ff = w_down.shape[1]
    tm, tc = TOKEN_TILE, FF_TILE
    nc = d_ff // tc
    n_ctx_tiles = n_ctx_tok // tm
    mrow = _mod_row_fn(layer, tm, n_ctx_tok, lat_len)
    assert nc >= 2 and lat_len == tm
    kern = functools.partial(_ffn_kernel, n_ctx_tiles=n_ctx_tiles, ctx_len=ctx_len, n_chunks=nc,
                             final_norm=final_norm)
    up_c = lambda c: jnp.minimum(c, nc - 1)
    down_c = lambda c: jnp.maximum(c - 1, 0)
    if final_norm:
        out_shape = (jax.ShapeDtypeStruct((n_ctx_tok, D), F32), jax.ShapeDtypeStruct((n_tok - n_ctx_tok, D), F32))
        out_specs = (pl.BlockSpec((tm, D), lambda i, c: (jnp.minimum(i, n_ctx_tiles - 1), 0)),
                     pl.BlockSpec((tm, D), lambda i, c: (jnp.maximum(i - n_ctx_tiles, 0), 0)))
    else:
        out_shape = jax.ShapeDtypeStruct((n_tok, D), F32)
        out_specs = pl.BlockSpec((tm, D), lambda i, c: (i, 0))
    return pl.pallas_call(
        kern,
        out_shape=out_shape,
        grid=(n_tok // tm, nc + 1),
        in_specs=[
            pl.BlockSpec((tm, D), lambda i, c: (i, 0)),
            pl.BlockSpec((1, 1, 6 * D), lambda i, c: (mrow(i), 0, 0)),
            pl.BlockSpec((1, D, tc), lambda i, c: (layer, 0, up_c(c))),
            pl.BlockSpec((1, D, tc), lambda i, c: (layer, 0, nc + up_c(c))),
            pl.BlockSpec((1, 3, tc), lambda i, c: (layer, 0, down_c(c))),
            pl.BlockSpec((1, 3, tc), lambda i, c: (layer, 0, nc + down_c(c))),
            pl.BlockSpec((1, 1, tc), lambda i, c: (layer, 0, down_c(c))),
            pl.BlockSpec((1, 1, tc), lambda i, c: (layer, 0, nc + down_c(c))),
            pl.BlockSpec((1, tc, D), lambda i, c: (layer, down_c(c), 0)),
            pl.BlockSpec((1, D), lambda i, c: (0, 0)),
        ],
        out_specs=out_specs,
        scratch_shapes=[pltpu.VMEM((tm, D), BF16), pltpu.VMEM((tm, D), F32),
                        pltpu.VMEM((2, tm, tc), F32), pltpu.VMEM((2, tm, tc), F32)],
        compiler_params=_cparams(("arbitrary", "arbitrary")),
        name="ffn_final" if final_norm else "ffn",
    )(x, mod, w_up, w_up, conv_w, conv_w, conv_b, conv_b, w_down, final_g)


def _rope_tables(n_pos):
    t = jnp.arange(n_pos)
    nf = DA_HD // 4
    inv = ROPE_THETA ** (-jnp.arange(nf, dtype=F32) / nf)
    ang = [(t // GRID_W).astype(F32)[:, None] * inv, (t % GRID_W).astype(F32)[:, None] * inv]
    zeros = jnp.zeros((n_pos, nf), F32)

    def lanes(first, second):
        sub = jnp.concatenate([first(ang[0]), second(ang[0]), first(ang[1]), second(ang[1])], axis=1)
        return jnp.concatenate([sub, sub], axis=1)

    cos = lanes(jnp.cos, jnp.cos)
    sin_a = lanes(lambda a: -jnp.sin(a), lambda a: zeros)
    sin_b = lanes(lambda a: zeros, jnp.sin)
    return cos, sin_a, sin_b


def kernel(x_prompt, x_sample, c, cache_k, cache_v, state_C, state_n, state_m, c_ctx, w_mod, b_mod, w_in,
           da_lambda, da_norm_g, ml_conv_w, ml_conv_b, ml_gate_b, ml_norm_g, sg_norm_g, sg_w, sg_b,
           w_branch, w_out, w_up, ffn_conv_w, ffn_conv_b, w_down, final_g):
    B, S, D = x_prompt.shape
    Bd, Sd, _ = x_sample.shape
    L = w_mod.shape[0]
    n_ctx_tok = B * S
    n_streams = 2 * ML_HEADS

    x = jnp.concatenate([x_prompt.reshape(n_ctx_tok, D), x_sample.reshape(Bd * Sd, D)], axis=0)
    cond8 = jnp.concatenate([c_ctx[None, :], c, jnp.zeros((8 - 1 - Bd, D), F32)], axis=0)
    mod = _modulation(cond8, w_mod, b_mod).reshape(L * 8, 1, 6 * D)
    rope_tabs = _rope_tables(Sd)

    wz, wg = _prep_w_in(w_in)
    gate_b = jnp.pad(ml_gate_b.reshape(L, 1, -1), ((0, 0), (0, 0), (0, 128 - N_GATE_COLS)))
    sg_bias = jnp.repeat(jnp.swapaxes(sg_b, 1, 2), BR_W // SG_GROUPS, axis=2)
    sg_w16 = sg_w.astype(BF16)
    w_branch16 = w_branch.astype(BF16)
    w_out16 = w_out.astype(BF16)
    row = lambda p: p.reshape(L, 1, -1)

    caches, states = (), ()
    for l in range(L):
        lam_init = 0.8 - 0.6 * math.exp(-0.3 * l)
        z, gates, kt = _inproj(x, mod, wz, wg, ml_conv_w, row(ml_conv_b), rope_tabs, layer=l,
                               n_ctx_tok=n_ctx_tok, ctx_len=S, lat_len=Sd)

        y_da, *caches = _attn_ctx(z, da_lambda, row(da_norm_g), tuple(caches), layer=l, n_layers=L,
                                  n_seq=B, seq_len=S, lam_init=lam_init)
        y_da = _attn_lat(z, cache_k, cache_v, da_lambda, row(da_norm_g), y_da, layer=l,
                         n_ctx_tok=n_ctx_tok, n_seq=Bd, seq_len=Sd, lam_init=lam_init)

        y_ml, *states = _mlstm(z, kt, gates, gate_b, row(ml_norm_g), None, tuple(states), layer=l,
                               n_layers=L, tok0=0, n_seq=B, seq_len=S, emit_state=True)
        m0 = jnp.broadcast_to(state_m[:, l].reshape(Bd, n_streams, 1), (Bd, n_streams, 128))
        (y_ml,) = _mlstm(z, kt, gates, gate_b, row(ml_norm_g), (state_C, state_n, m0), (y_ml,), layer=l,
                         n_layers=L, tok0=n_ctx_tok, n_seq=Bd, seq_len=Sd, emit_state=False)

        x = _merge(x, mod, y_da, y_ml, z, row(sg_norm_g), sg_w16, sg_bias, w_branch16, w_out16,
                   layer=l, n_ctx_tok=n_ctx_tok, lat_len=Sd)
        x = _ffn(x, mod, w_up, ffn_conv_w, row(ffn_conv_b), w_down, final_g.reshape(1, -1), layer=l,
                 n_ctx_tok=n_ctx_tok, ctx_len=S, lat_len=Sd, final_norm=(l == L - 1))

    y_prompt, y_sample = x
    new_k, new_v = caches
    new_C, new_n, new_m = states
    return (y_prompt.reshape(B, S, D), y_sample.reshape(Bd, Sd, D), new_k, new_v,
            new_C.reshape(B, L, 2, ML_HEADS, ML_HD, ML_HD), new_n.reshape(B, L, 2, ML_HEADS, ML_HD),
            new_m[:, :, :, 0].reshape(B, L, 2, ML_HEADS))
```

```python
import functools
import math

import jax
import jax.numpy as jnp
from jax import lax
from jax.experimental import pallas as pl
from jax.experimental.pallas import tpu as pltpu

F32 = jnp.float32
BF16 = jnp.bfloat16

GRID_W = 64
DA_HEADS = 4
DA_HD = 64
ML_HEADS = 4
ML_HD = 128
CHUNK = 128
SG_GROUPS = 4
BR_W = 512
N_BRANCH = 3
ROPE_THETA = 10000.0
EPS = 1e-6
NEG = -1e30

VMEM_LIMIT_BYTES = 52 * 1024 * 1024

TOKEN_TILE = 1024
COL_TILE = 512
MERGE_TILE = 512
FF_TILE = 256
Q_TILE = 256

ZJ_Q, ZJ_K, ZJ_V, ZJ_MQ, ZJ_MK, ZJ_MV, ZJ_MO, ZJ_SGU = range(8)
ZJ_GATE0 = 8
ZJ_SGV = 14
N_ZJ = 15


def _cparams(sem):
    return pltpu.CompilerParams(dimension_semantics=sem, vmem_limit_bytes=VMEM_LIMIT_BYTES)


def _dot(a, b):
    return jnp.dot(a, b, preferred_element_type=F32)


def _dot_nt(a, b):
    return lax.dot_general(a, b, (((1,), (1,)), ((), ())), preferred_element_type=F32)


def _dot_tn(a, b):
    return lax.dot_general(a, b, (((0,), (0,)), ((), ())), preferred_element_type=F32)


def _mod_kernel(cond_ref, w_ref, b_ref, o_ref):
    a = jax.nn.silu(cond_ref[...]).astype(BF16)
    o_ref[0] = _dot(a, w_ref[0].astype(BF16)) + b_ref[0]


def _modulation(cond8, w_mod, b_mod):
    L, D, N = w_mod.shape
    tn = 768
    return pl.pallas_call(
        _mod_kernel,
        out_shape=jax.ShapeDtypeStruct((L, 8, N), F32),
        grid=(L, N // tn),
        in_specs=[
            pl.BlockSpec((8, D), lambda l, j: (0, 0)),
            pl.BlockSpec((1, D, tn), lambda l, j: (l, 0, j)),
            pl.BlockSpec((1, 1, tn), lambda l, j: (l, 0, j)),
        ],
        out_specs=pl.BlockSpec((1, 8, tn), lambda l, j: (l, 0, j)),
        compiler_params=_cparams(("arbitrary", "arbitrary")),
        name="modulation",
    )(cond8, w_mod, b_mod.reshape(L, 1, N))


N_GATE_COLS = 4 * ML_HEADS
SRC_MLG_TILE = 7


def _wprep_kernel(a_ref, b_ref, wz_ref, wg_ref):
    t = pl.program_id(1)

    @pl.when(t < SRC_MLG_TILE)
    def _():
        wz_ref[0] = a_ref[0].astype(BF16)

    @pl.when(t >= SRC_MLG_TILE)
    def _():
        both = jnp.concatenate([a_ref[0], b_ref[0]], axis=1)
        width = both.shape[1]
        wz_ref[0] = pltpu.roll(both, width - N_GATE_COLS, axis=1)[:, :COL_TILE].astype(BF16)

    @pl.when(t == ZJ_SGU)
    def _():
        lane = lax.broadcasted_iota(jnp.int32, (1, 128), 1)
        wg_ref[0] = jnp.where(lane < N_GATE_COLS, a_ref[0, :, 0:128], 0.0).astype(BF16)


def _prep_w_in(w_in):
    L, D, n_in = w_in.shape
    assert n_in == N_ZJ * COL_TILE + N_GATE_COLS

    def src_tile(t):
        return jnp.where(t <= ZJ_SGU, t, jnp.where(t == ZJ_SGV, ZJ_SGU + 1, t + 1))

    lanes_per_tile = COL_TILE // 128
    return pl.pallas_call(
        _wprep_kernel,
        out_shape=(jax.ShapeDtypeStruct((L, D, N_ZJ * COL_TILE), BF16),
                   jax.ShapeDtypeStruct((L, D, 128), BF16)),
        grid=(L, N_ZJ),
        in_specs=[pl.BlockSpec((1, D, COL_TILE), lambda l, t: (l, 0, src_tile(t))),
                  pl.BlockSpec((1, D, 128), lambda l, t: (l, 0, (src_tile(t) + 1) * lanes_per_tile))],
        out_specs=(pl.BlockSpec((1, D, COL_TILE), lambda l, t: (l, 0, t)),
                   pl.BlockSpec((1, D, 128), lambda l, t: (l, 0, 0))),
        compiler_params=_cparams(("arbitrary", "arbitrary")),
        name="w_in_prep",
    )(w_in, w_in)


def _rms(x):
    return x * lax.rsqrt(jnp.mean(x * x, axis=-1, keepdims=True) + EPS)


def _sigmoid(x):
    return 0.5 * jnp.tanh(0.5 * x) + 0.5


def _silu(x):
    return x * _sigmoid(x)


def _seq_conv3(u, w_ref, b_ref, is_lat, ctx_len):
    rows = u.shape[0]
    prev = pltpu.roll(u, 1, axis=0)
    nxt = pltpu.roll(u, rows - 1, axis=0)
    r8 = lax.broadcasted_iota(jnp.int32, (8, 1), 0)
    interior = jnp.where(is_lat, 1.0, 0.0).astype(F32)
    n_pieces = rows // ctx_len
    pp, nn = [], []
    for p in range(n_pieces):
        lo, hi = p * ctx_len, (p + 1) * ctx_len
        keep_first = interior if p > 0 else 0.0
        keep_last = interior if p < n_pieces - 1 else 0.0
        pp += [prev[lo:lo + 8] * jnp.where(r8 == 0, keep_first, 1.0), prev[lo + 8:hi]]
        nn += [nxt[lo:hi - 8], nxt[hi - 8:hi] * jnp.where(r8 == 7, keep_last, 1.0)]
    prev = jnp.concatenate(pp, axis=0)
    nxt = jnp.concatenate(nn, axis=0)
    return prev * w_ref[0, 0:1, :] + u * w_ref[0, 1:2, :] + nxt * w_ref[0, 2:3, :] + b_ref[0]


N_POS = N_ZJ


def _tile_at(p):
    k = p // 2
    even = jnp.where(k < 6, k, jnp.where(k == 6, ZJ_SGU, ZJ_SGV))
    odd = jnp.where(k == 0, ZJ_MO, ZJ_GATE0 - 1 + k)
    return jnp.where(p % 2 == 0, even, odd)


POS_MK = 2 * ZJ_MK


def _inproj_kernel(x_ref, mod_ref, wz_ref, wg_ref, cw_ref, cb_ref, cos_ref, sa_ref, sb_ref,
                   z_ref, g_ref, kt_ref, h_scr, zb_even, zb_odd, *, n_ctx_tiles, ctx_len):
    i = pl.program_id(0)
    p = pl.program_id(1)
    D = x_ref.shape[1]
    is_lat = i >= n_ctx_tiles
    is_ctx = jnp.logical_not(is_lat)
    k = (p - 1) // 2

    def product(dst):
        dst[...] = _dot(h_scr[...], wz_ref[0])

    def rope(zz):
        parts = []
        for hd in range(zz.shape[1] // 128):
            xh = zz[:, hd * 128:(hd + 1) * 128]
            parts.append(xh * cos_ref[...] + pltpu.roll(xh, 112, axis=1) * sa_ref[...]
                         + pltpu.roll(xh, 16, axis=1) * sb_ref[...])
        return jnp.concatenate(parts, axis=1)

    @pl.when(p == 0)
    def _():
        h = _rms(x_ref[...]) * (1.0 + mod_ref[0, :, D:2 * D]) + mod_ref[0, :, 0:D]
        hb = h.astype(BF16)
        h_scr[...] = hb
        g_ref[...] = _dot(hb, wg_ref[0])
        product(zb_even)

    @pl.when((p % 2 == 0) & (p > 0))
    def _():
        z_ref[...] = _sigmoid(zb_odd[...]).astype(BF16)
        product(zb_even)

    odd = p % 2 == 1

    @pl.when(odd & (k == ZJ_Q) & is_lat)
    def _():
        z_ref[...] = (rope(zb_even[...]) * (DA_HD ** -0.5)).astype(BF16)
        product(zb_odd)

    @pl.when(odd & (k == ZJ_K) & is_lat)
    def _():
        z_ref[...] = rope(zb_even[...]).astype(BF16)
        product(zb_odd)

    @pl.when(odd & ((((k == ZJ_Q) | (k == ZJ_K)) & is_ctx) | (k == ZJ_V) | (k == ZJ_MV)))
    def _():
        scale = jnp.where(k == ZJ_Q, DA_HD ** -0.5, 1.0).astype(F32)
        z_ref[...] = (zb_even[...] * scale).astype(BF16)
        product(zb_odd)

    @pl.when(odd & (k == ZJ_MQ))
    def _():
        z_ref[...] = _silu(_seq_conv3(zb_even[...], cw_ref, cb_ref, is_lat, ctx_len)).astype(BF16)
        product(zb_odd)

    @pl.when(odd & (k == ZJ_MK))
    def _():
        a = _silu(_seq_conv3(zb_even[...], cw_ref, cb_ref, is_lat, ctx_len)) * (ML_HD ** -0.5)
        kt_ref[...] = a.T.astype(BF16)
        product(zb_odd)

    @pl.when(odd & (k == 6))
    def _():
        z_ref[...] = jax.nn.gelu(zb_even[...], approximate=True).astype(BF16)
        product(zb_odd)

    @pl.when(odd & (k == 7))
    def _():
        z_ref[...] = jax.nn.gelu(zb_even[...], approximate=True).astype(BF16)


def _inproj(x, mod, wz, wg, conv_w, conv_b, rope_tabs, *, layer, n_ctx_tok, ctx_len, lat_len):
    n_tok, D = x.shape
    tm, tn = TOKEN_TILE, COL_TILE
    n_ctx_tiles = n_ctx_tok // tm
    assert lat_len == tm and tm % ctx_len == 0 and n_ctx_tok % tm == 0

    def mod_row(i):
        return layer * 8 + jnp.where(i < n_ctx_tiles, 0, i - n_ctx_tiles + 1)

    def conv_col(p):
        return jnp.where(p > POS_MK, 1, 0)

    def w_tile(p):
        return _tile_at(jnp.minimum(p, N_POS - 1))

    def z_tile(p):
        fin = jnp.maximum(p - 1, 0)
        return _tile_at(jnp.where(fin == POS_MK, fin - 1, fin))

    kern = functools.partial(_inproj_kernel, n_ctx_tiles=n_ctx_tiles, ctx_len=ctx_len)
    return pl.pallas_call(
        kern,
        out_shape=(jax.ShapeDtypeStruct((n_tok, N_ZJ * tn), BF16),
                   jax.ShapeDtypeStruct((n_tok, 128), F32),
                   jax.ShapeDtypeStruct((tn, n_tok), BF16)),
        grid=(n_tok // tm, N_POS + 1),
        in_specs=[
            pl.BlockSpec((tm, D), lambda i, p: (i, 0)),
            pl.BlockSpec((1, 1, 2 * D), lambda i, p: (mod_row(i), 0, 0)),
            pl.BlockSpec((1, D, tn), lambda i, p: (layer, 0, w_tile(p))),
            pl.BlockSpec((1, D, 128), lambda i, p: (layer, 0, 0)),
            pl.BlockSpec((1, 3, tn), lambda i, p: (layer, 0, conv_col(p))),
            pl.BlockSpec((1, 1, tn), lambda i, p: (layer, 0, conv_col(p))),
            pl.BlockSpec((tm, 128), lambda i, p: (0, 0)),
            pl.BlockSpec((tm, 128), lambda i, p: (0, 0)),
            pl.BlockSpec((tm, 128), lambda i, p: (0, 0)),
        ],
        out_specs=(pl.BlockSpec((tm, tn), lambda i, p: (i, z_tile(p))),
                   pl.BlockSpec((tm, 128), lambda i, p: (i, 0)),
                   pl.BlockSpec((tn, tm), lambda i, p: (0, i))),
        scratch_shapes=[pltpu.VMEM((tm, D), BF16), pltpu.VMEM((tm, tn), F32), pltpu.VMEM((tm, tn), F32)],
        compiler_params=_cparams(("arbitrary", "arbitrary")),
        name="inproj",
    )(x, mod, wz, wg, conv_w, conv_b, *rope_tabs)


def _diff_attn_core(q, ks, vs, lam, gain):
    lane = lax.broadcasted_iota(jnp.int32, (1, 128), 1)
    lo = lane < DA_HD
    zero = jnp.zeros_like(q)
    q1 = jnp.where(lo, q, zero)
    q2 = jnp.where(lo, zero, q)

    def softmax_times_v(qh):
        s = [_dot_nt(qh, k) for k in ks]
        mx = functools.reduce(jnp.maximum, [jnp.max(t, axis=-1, keepdims=True) for t in s])
        e = [jnp.exp(t - mx) for t in s]
        den = functools.reduce(jnp.add, [jnp.sum(t, axis=-1, keepdims=True) for t in e])
        o = functools.reduce(jnp.add, [_dot(t.astype(BF16), v) for t, v in zip(e, vs)])
        return o, 1.0 / den

    o1, r1 = softmax_times_v(q1)
    o2, r2 = softmax_times_v(q2)
    return _rms(o1 * r1 - o2 * (lam * r2)) * gain


def _lambda_full(lam_ref, lam_init):
    p = lam_ref[0]
    s1 = jnp.sum(p[0:1, :] * p[1:2, :], axis=-1, keepdims=True)
    s2 = jnp.sum(p[2:3, :] * p[3:4, :], axis=-1, keepdims=True)
    return jnp.exp(s1) - jnp.exp(s2) + lam_init


def _carried(arrays):
    return [pl.BlockSpec(memory_space=pl.ANY)] * len(arrays)


def _attn_ctx_kernel(*refs, lam_init, n_carried):
    q_ref, k_ref, v_ref, lam_ref, g_ref = refs[:5]
    y_ref, ck_ref, cv_ref = refs[5 + n_carried:]
    lam = _lambda_full(lam_ref, lam_init)
    for h in range(DA_HEADS):
        hs = slice(h * 128, (h + 1) * 128)
        k = k_ref[:, hs]
        v = v_ref[:, hs]
        o = _diff_attn_core(q_ref[:, hs], [k], [v], lam, g_ref[0])
        y_ref[:, hs] = (o * (1.0 - lam_init)).astype(BF16)
        ck_ref[0, 0, h] = k.astype(F32)
        cv_ref[0, 0, h] = v.astype(F32)


def _attn_ctx(z, da_lam, norm_g, carried, *, layer, n_layers, n_seq, seq_len, lam_init):
    n_tok = z.shape[0]
    kern = functools.partial(_attn_ctx_kernel, lam_init=lam_init, n_carried=len(carried))
    blk = lambda zj: pl.BlockSpec((seq_len, COL_TILE), lambda b: (b, zj))
    cache_sds = jax.ShapeDtypeStruct((n_seq, n_layers, DA_HEADS, seq_len, 128), F32)
    cache_blk = pl.BlockSpec((1, 1, DA_HEADS, seq_len, 128), lambda b: (b, layer, 0, 0, 0))
    return pl.pallas_call(
        kern,
        out_shape=(jax.ShapeDtypeStruct((n_tok, BR_W), BF16), cache_sds, cache_sds),
        grid=(n_seq,),
        in_specs=[blk(ZJ_Q), blk(ZJ_K), blk(ZJ_V),
                  pl.BlockSpec((1, 4, DA_HD), lambda b: (layer, 0, 0)),
                  pl.BlockSpec((1, 1, 128), lambda b: (layer, 0, 0))] + _carried(carried),
        out_specs=(pl.BlockSpec((seq_len, BR_W), lambda b: (b, 0)), cache_blk, cache_blk),
        input_output_aliases={5 + n: 1 + n for n in range(len(carried))},
        compiler_params=_cparams(("arbitrary",)),
        name="attn_ctx",
    )(z, z, z, da_lam, norm_g, *carried)


def _attn_lat_kernel(q_ref, k_ref, v_ref, ck_ref, cv_ref, lam_ref, g_ref, yin_ref, y_ref, *, lam_init):
    del yin_ref
    lam = _lambda_full(lam_ref, lam_init)
    ks = [ck_ref[0, 0, 0].astype(BF16), k_ref[...]]
    vs = [cv_ref[0, 0, 0].astype(BF16), v_ref[...]]
    for t in range(q_ref.shape[0] // Q_TILE):
        rows = slice(t * Q_TILE, (t + 1) * Q_TILE)
        o = _diff_attn_core(q_ref[rows, :], ks, vs, lam, g_ref[0])
        y_ref[rows, :] = (o * (1.0 - lam_init)).astype(BF16)


def _attn_lat(z, cache_k, cache_v, da_lam, norm_g, y_da, *, layer, n_ctx_tok, n_seq, seq_len, lam_init):
    hb = COL_TILE // 128
    s0 = n_ctx_tok // seq_len
    past = cache_k.shape[3]
    kern = functools.partial(_attn_lat_kernel, lam_init=lam_init)
    zblk = lambda zj: pl.BlockSpec((seq_len, 128), lambda b, h: (s0 + b, zj * hb + h))
    return pl.pallas_call(
        kern,
        out_shape=jax.ShapeDtypeStruct(y_da.shape, y_da.dtype),
        grid=(n_seq, DA_HEADS),
        in_specs=[
            zblk(ZJ_Q), zblk(ZJ_K), zblk(ZJ_V),
            pl.BlockSpec((1, 1, 1, past, 128), lambda b, h: (b, layer, h, 0, 0)),
            pl.BlockSpec((1, 1, 1, past, 128), lambda b, h: (b, layer, h, 0, 0)),
            pl.BlockSpec((1, 4, DA_HD), lambda b, h: (layer, 0, 0)),
            pl.BlockSpec((1, 1, 128), lambda b, h: (layer, 0, 0)),
            pl.BlockSpec(memory_space=pl.ANY),
        ],
        out_specs=pl.BlockSpec((seq_len, 128), lambda b, h: (s0 + b, h)),
        input_output_aliases={7: 0},
        compiler_params=_cparams(("arbitrary", "arbitrary")),
        name="attn_lat",
    )(z, z, z, cache_k, cache_v, da_lam, norm_g, y_da)


def _split3(x):
    hi = x.astype(BF16)
    r = x - hi.astype(F32)
    mid = r.astype(BF16)
    lo = (r - mid.astype(F32)).astype(BF16)
    return hi, mid, lo


def _mlstm_kernel(*refs, n_chunks, has_init, emit_state, n_carried):
    q_ref, kt_ref, v_ref, og_ref, gt_ref, gb_ref, ng_ref = refs[:7]
    pos = 7
    if has_init:
        c0_ref, n0_ref, m0_ref = refs[pos:pos + 3]
        pos += 3
    pos += n_carried
    y_ref = refs[pos]
    pos += 1
    if emit_state:
        co_ref, no_ref, mo_ref = refs[pos:pos + 3]
        pos += 3
    h_fw, h_bw, cn_s, m_s = refs[pos:pos + 4]
    hdir = (h_fw, h_bw)

    n_streams = 2 * ML_HEADS
    for s in range(n_streams):
        d, h = divmod(s, ML_HEADS)
        if has_init:
            n_col = jnp.broadcast_to(n0_ref[0, 0, d, h:h + 1, :], (ML_HD, ML_HD)).T
            cn_s[s] = jnp.concatenate([c0_ref[0, 0, d, h], n_col], axis=1)
            m_s[s] = m0_ref[0, s:s + 1, :]
        else:
            cn_s[s] = jnp.zeros((ML_HD, 2 * ML_HD), F32)
            m_s[s] = jnp.zeros((1, 128), F32)

    row = lax.broadcasted_iota(jnp.int32, (CHUNK, CHUNK), 0)
    col = lax.broadcasted_iota(jnp.int32, (CHUNK, CHUNK), 1)
    masks = (col <= row, col >= row)
    tris = tuple(jnp.where(m, 1.0, 0.0).astype(BF16) for m in masks)
    last_row = (CHUNK - 1, 0)

    ones = jnp.ones((CHUNK, ML_HD), BF16)

    def step(c_fw):
        rows_d, prep = [], []
        for d in range(2):
            c = c_fw if d == 0 else n_chunks - 1 - c_fw
            r0 = c * CHUNK
            if not isinstance(r0, int):
                r0 = pl.multiple_of(r0, CHUNK)
            rows = pl.ds(r0, CHUNK)
            pre = gt_ref[rows, :] + gb_ref[0]
            logf = jax.nn.log_sigmoid(pre)
            hi, mid, lo = _split3(logf)
            csum = _dot(tris[d], hi) + _dot(tris[d], mid) + _dot(tris[d], lo)
            rows_d.append(rows)
            prep.append((csum, pre.T, csum.T))

        streams = [(d, h) for d in range(2) for h in range(ML_HEADS)]
        st = []
        for d, h in streams:
            s = d * ML_HEADS + h
            csum, pre_t, csum_t = prep[d]
            rows = rows_d[d]
            li, lf = d * ML_HEADS + h, 2 * ML_HEADS + d * ML_HEADS + h
            hs = slice(h * ML_HD, (h + 1) * ML_HD)
            b_t = jnp.broadcast_to(csum[:, lf:lf + 1], (CHUNK, 128))
            r_row = pre_t[li:li + 1, :] - csum_t[lf:lf + 1, :]
            m_prev = m_s[s]
            qc = q_ref[rows, hs]
            ktc = kt_ref[hs, rows]
            log_w = jnp.where(masks[d], b_t + r_row, NEG)
            inter = b_t + m_prev
            m_t = jnp.maximum(inter, jnp.broadcast_to(jnp.max(log_w, axis=-1, keepdims=True),
                                                      (CHUNK, 128)))
            st.append(dict(s=s, d=d, rows=rows, hs=hs, b_t=b_t, r_row=r_row, m_prev=m_prev, qc=qc,
                           ktc=ktc, log_w=log_w, inter=inter, m_t=m_t, qkt=_dot(qc, ktc),
                           qcn=_dot(qc, cn_s[s].astype(BF16))))

        for e in st:
            e["v_ext"] = jnp.concatenate([v_ref[e["rows"], e["hs"]], ones], axis=1)
            m_t = e["m_t"]
            qk = (e["qkt"] * jnp.exp(e["log_w"] - m_t)).astype(BF16)
            s_inter = jnp.exp(e["inter"] - m_t)
            both = _dot(qk, e["v_ext"]) + jnp.concatenate([s_inter, s_inter], axis=1) * e["qcn"]
            num, den = both[:, :ML_HD], both[:, ML_HD:]
            hdir[e["d"]][e["rows"], e["hs"]] = num / jnp.maximum(jnp.abs(den), jnp.exp(-m_t))

        for e in st:
            s, lr = e["s"], last_row[e["d"]]
            m_new = e["m_t"][lr:lr + 1, :]
            b_last = e["b_t"][lr:lr + 1, :]
            g_row = jnp.exp(b_last + e["r_row"] - m_new)
            decay = jnp.exp(b_last + e["m_prev"] - m_new)
            gkt = (e["ktc"].astype(F32) * g_row).astype(BF16)
            cn_s[s] = jnp.concatenate([decay, decay], axis=1) * cn_s[s] + _dot(gkt, e["v_ext"])
            m_s[s] = m_new

    if n_chunks <= 2:
        for c in range(n_chunks):
            step(c)
    else:
        def body(c, carry):
            step(c)
            return carry
        lax.fori_loop(0, n_chunks, body, 0)

    for c in range(n_chunks):
        rows = slice(c * CHUNK, (c + 1) * CHUNK)
        for h in range(ML_HEADS):
            hs = slice(h * ML_HD, (h + 1) * ML_HD)
            hn = _rms(h_fw[rows, hs] + h_bw[rows, hs]) * ng_ref[0]
            y_ref[rows, hs] = (og_ref[rows, hs].astype(F32) * hn.astype(F32)).astype(BF16)

    if emit_state:
        for s in range(n_streams):
            cn = cn_s[s]
            co_ref[0, 0, s] = cn[:, :ML_HD]
            no_ref[0, 0, s:s + 1, :] = cn[:, ML_HD:].T[0:1, :]
            mo_ref[0, 0, s:s + 1, :] = m_s[s]


def _mlstm(z, kt, gates, gate_b, norm_g, init, carried, *, layer, n_layers, tok0, n_seq, seq_len, emit_state):
    n_tok = z.shape[0]
    s0 = tok0 // seq_len
    n_chunks = seq_len // CHUNK
    has_init = init is not None
    n_streams = 2 * ML_HEADS
    zblk = lambda zj: pl.BlockSpec((seq_len, BR_W), lambda b: (s0 + b, zj))
    in_specs = [zblk(ZJ_MQ), pl.BlockSpec((BR_W, seq_len), lambda b: (0, s0 + b)), zblk(ZJ_MV), zblk(ZJ_MO),
                pl.BlockSpec((seq_len, 128), lambda b: (s0 + b, 0)),
                pl.BlockSpec((1, 1, 128), lambda b: (layer, 0, 0)),
                pl.BlockSpec((1, 1, 128), lambda b: (layer, 0, 0))]
    args = [z, kt, z, z, gates, gate_b, norm_g]
    if has_init:
        c0, n0, m0 = init
        in_specs += [
            pl.BlockSpec((1, 1, 2, ML_HEADS, ML_HD, ML_HD), lambda b: (b, layer, 0, 0, 0, 0)),
            pl.BlockSpec((1, 1, 2, ML_HEADS, ML_HD), lambda b: (b, layer, 0, 0, 0)),
            pl.BlockSpec((1, n_streams, 128), lambda b: (b, 0, 0)),
        ]
        args += [c0, n0, m0]
    first_carried = len(args)
    in_specs += _carried(carried)
    args += list(carried)
    out_shape = [jax.ShapeDtypeStruct((n_tok, BR_W), BF16)]
    out_specs = [pl.BlockSpec((seq_len, BR_W), lambda b: (s0 + b, 0))]
    if emit_state:
        out_shape += [jax.ShapeDtypeStruct((n_seq, n_layers, n_streams, ML_HD, ML_HD), F32),
                      jax.ShapeDtypeStruct((n_seq, n_layers, n_streams, ML_HD), F32),
                      jax.ShapeDtypeStruct((n_seq, n_layers, n_streams, 128), F32)]
        out_specs += [pl.BlockSpec((1, 1, n_streams, ML_HD, ML_HD), lambda b: (b, layer, 0, 0, 0)),
                      pl.BlockSpec((1, 1, n_streams, ML_HD), lambda b: (b, layer, 0, 0)),
                      pl.BlockSpec((1, 1, n_streams, 128), lambda b: (b, layer, 0, 0))]
        aliases = {first_carried + n: 1 + n for n in range(len(carried))}
    else:
        aliases = {first_carried: 0}
    kern = functools.partial(_mlstm_kernel, n_chunks=n_chunks, has_init=has_init, emit_state=emit_state,
                             n_carried=len(carried))
    return pl.pallas_call(
        kern,
        out_shape=tuple(out_shape),
        grid=(n_seq,),
        in_specs=in_specs,
        out_specs=tuple(out_specs),
        input_output_aliases=aliases,
        scratch_shapes=[pltpu.VMEM((seq_len, BR_W), F32), pltpu.VMEM((seq_len, BR_W), F32),
                        pltpu.VMEM((n_streams, ML_HD, 2 * ML_HD), F32),
                        pltpu.VMEM((n_streams, 1, 128), F32)],
        compiler_params=_cparams(("arbitrary",)),
        name="mlstm_init" if has_init else "mlstm_zero",
    )(*args)


def _merge_kernel(x_ref, mod_ref, yda_ref, yml_ref, sgu_ref, sgv_ref, g0_ref, g1_ref, g2_ref,
                  sgn_ref, sgw_ref, sgb_ref, wb_ref, wo_ref, o_ref):
    D = x_ref.shape[1]
    tm = x_ref.shape[0]
    v = sgv_ref[...].astype(F32)
    vc = v - jnp.mean(v, axis=-1, keepdims=True)
    sv = (vc * lax.rsqrt(jnp.mean(vc * vc, axis=-1, keepdims=True) + EPS) * sgn_ref[0]).astype(BF16)
    mixed = []
    for c in range(tm // CHUNK):
        rows = slice(c * CHUNK, (c + 1) * CHUNK)
        groups = [_dot(sgw_ref[0, g], sv[rows, g * 128:(g + 1) * 128]) for g in range(SG_GROUPS)]
        mixed.append(jnp.concatenate(groups, axis=1) + sgb_ref[0])
    y_sg = (sgu_ref[...].astype(F32) * jnp.concatenate(mixed, axis=0)).astype(BF16)

    m = g0_ref[...].astype(F32) * _dot(yda_ref[...], wb_ref[0, 0])
    m = m + g1_ref[...].astype(F32) * _dot(yml_ref[...], wb_ref[0, 1])
    m = m + g2_ref[...].astype(F32) * _dot(y_sg, wb_ref[0, 2])
    out = _dot(m.astype(BF16), wo_ref[0])
    o_ref[...] = x_ref[...] + mod_ref[0, :, 2 * D:3 * D] * out


def _mod_row_fn(layer, tile, n_ctx_tok, lat_len):
    def f(i):
        t0 = i * tile
        return layer * 8 + jnp.where(t0 < n_ctx_tok, 0, 1 + (t0 - n_ctx_tok) // lat_len)
    return f


def _merge(x, mod, y_da, y_ml, z, sg_norm_g, sg_w, sg_bias, w_branch, w_out, *, layer, n_ctx_tok, lat_len):
    n_tok, D = x.shape
    tm = MERGE_TILE
    mrow = _mod_row_fn(layer, tm, n_ctx_tok, lat_len)
    gate_blk = lambda n: pl.BlockSpec((tm, D), lambda i: (i, ZJ_GATE0 * COL_TILE // D + n))
    return pl.pallas_call(
        _merge_kernel,
        out_shape=jax.ShapeDtypeStruct((n_tok, D), F32),
        grid=(n_tok // tm,),
        in_specs=[
            pl.BlockSpec((tm, D), lambda i: (i, 0)),
            pl.BlockSpec((1, 1, 6 * D), lambda i: (mrow(i), 0, 0)),
            pl.BlockSpec((tm, BR_W), lambda i: (i, 0)),
            pl.BlockSpec((tm, BR_W), lambda i: (i, 0)),
            pl.BlockSpec((tm, COL_TILE), lambda i: (i, ZJ_SGU)),
            pl.BlockSpec((tm, COL_TILE), lambda i: (i, ZJ_SGV)),
            gate_blk(0), gate_blk(1), gate_blk(2),
            pl.BlockSpec((1, 1, BR_W), lambda i: (layer, 0, 0)),
            pl.BlockSpec((1, SG_GROUPS, CHUNK, CHUNK), lambda i: (layer, 0, 0, 0)),
            pl.BlockSpec((1, CHUNK, BR_W), lambda i: (layer, 0, 0)),
            pl.BlockSpec((1, N_BRANCH, BR_W, D), lambda i: (layer, 0, 0, 0)),
            pl.BlockSpec((1, D, D), lambda i: (layer, 0, 0)),
        ],
        out_specs=pl.BlockSpec((tm, D), lambda i: (i, 0)),
        compiler_params=_cparams(("arbitrary",)),
        name="merge",
    )(x, mod, y_da, y_ml, z, z, z, z, z, sg_norm_g, sg_w, sg_bias, w_branch, w_out)


def _ffn_kernel(x_ref, mod_ref, wa_ref, wg_ref, cwa_ref, cwg_ref, cba_ref, cbg_ref, wd_ref, fg_ref,
                *rest, n_ctx_tiles, ctx_len, n_chunks, final_norm):
    out_refs, (h_scr, acc, u0, u1) = rest[:-4], rest[-4:]
    ubufs = (u0, u1)
    i = pl.program_id(0)
    c = pl.program_id(1)
    D = x_ref.shape[1]
    is_lat = i >= n_ctx_tiles

    def run(par, up, down):
        if up:
            h = h_scr[...]
            ubufs[par][0] = _dot(h, wa_ref[0].astype(BF16))
            ubufs[par][1] = _dot(h, wg_ref[0].astype(BF16))
        if down:
            ua = _seq_conv3(ubufs[1 - par][0], cwa_ref, cba_ref, is_lat, ctx_len)
            ug = _seq_conv3(ubufs[1 - par][1], cwg_ref, cbg_ref, is_lat, ctx_len)
            a = (_silu(ua) * ug).astype(BF16)
            acc[...] += _dot(a, wd_ref[0].astype(BF16))

    @pl.when(c == 0)
    def _():
        h = _rms(x_ref[...]) * (1.0 + mod_ref[0, :, 4 * D:5 * D]) + mod_ref[0, :, 3 * D:4 * D]
        h_scr[...] = h.astype(BF16)
        acc[...] = jnp.zeros_like(acc)
        run(0, True, False)

    steady = (c >= 1) & (c < n_chunks)
    for par in range(2):
        @pl.when(steady & (c % 2 == par))
        def _():
            run(par, True, True)

    last = c == n_chunks

    @pl.when(last)
    def _():
        run(n_chunks % 2, False, True)

    if final_norm:
        yp_ref, ys_ref = out_refs

        def result():
            return _rms(x_ref[...] + mod_ref[0, :, 5 * D:6 * D] * acc[...]) * fg_ref[...]

        @pl.when(last & jnp.logical_not(is_lat))
        def _():
            yp_ref[...] = result()

        @pl.when(last & is_lat)
        def _():
            ys_ref[...] = result()
    else:
        @pl.when(last)
        def _():
            out_refs[0][...] = x_ref[...] + mod_ref[0, :, 5 * D:6 * D] * acc[...]


def _ffn(x, mod, w_up, conv_w, conv_b, w_down, final_g, *, layer, n_ctx_tok, ctx_len, lat_len, final_norm):
    n_tok, D = x.shape
    d_ff = w_down.shape[1]
    tm, tc = TOKEN_TILE, FF_TILE
    nc = d_ff // tc
    n_ctx_tiles = n_ctx_tok // tm
    mrow = _mod_row_fn(layer, tm, n_ctx_tok, lat_len)
    assert nc >= 2 and lat_len == tm
    kern = functools.partial(_ffn_kernel, n_ctx_tiles=n_ctx_tiles, ctx_len=ctx_len, n_chunks=nc,
                             final_norm=final_norm)
    up_c = lambda c: jnp.minimum(c, nc - 1)
    down_c = lambda c: jnp.maximum(c - 1, 0)
    if final_norm:
        out_shape = (jax.ShapeDtypeStruct((n_ctx_tok, D), F32), jax.ShapeDtypeStruct((n_tok - n_ctx_tok, D), F32))
        out_specs = (pl.BlockSpec((tm, D), lambda i, c: (jnp.minimum(i, n_ctx_tiles - 1), 0)),
                     pl.BlockSpec((tm, D), lambda i, c: (jnp.maximum(i - n_ctx_tiles, 0), 0)))
    else:
        out_shape = jax.ShapeDtypeStruct((n_tok, D), F32)
        out_specs = pl.BlockSpec((tm, D), lambda i, c: (i, 0))
    return pl.pallas_call(
        kern,
        out_shape=out_shape,
        grid=(n_tok // tm, nc + 1),
        in_specs=[
            pl.BlockSpec((tm, D), lambda i, c: (i, 0)),
            pl.BlockSpec((1, 1, 6 * D), lambda i, c: (mrow(i), 0, 0)),
            pl.BlockSpec((1, D, tc), lambda i, c: (layer, 0, up_c(c))),
            pl.BlockSpec((1, D, tc), lambda i, c: (layer, 0, nc + up_c(c))),
            pl.BlockSpec((1, 3, tc), lambda i, c: (layer, 0, down_c(c))),
            pl.BlockSpec((1, 3, tc), lambda i, c: (layer, 0, nc + down_c(c))),
            pl.BlockSpec((1, 1, tc), lambda i, c: (layer, 0, down_c(c))),
            pl.BlockSpec((1, 1, tc), lambda i, c: (layer, 0, nc + down_c(c))),
            pl.BlockSpec((1, tc, D), lambda i, c: (layer, down_c(c), 0)),
            pl.BlockSpec((1, D), lambda i, c: (0, 0)),
        ],
        out_specs=out_specs,
        scratch_shapes=[pltpu.VMEM((tm, D), BF16), pltpu.VMEM((tm, D), F32),
                        pltpu.VMEM((2, tm, tc), F32), pltpu.VMEM((2, tm, tc), F32)],
        compiler_params=_cparams(("arbitrary", "arbitrary")),
        name="ffn_final" if final_norm else "ffn",
    )(x, mod, w_up, w_up, conv_w, conv_w, conv_b, conv_b, w_down, final_g)


def _rope_tables(n_pos):
    t = jnp.arange(n_pos)
    nf = DA_HD // 4
    inv = ROPE_THETA ** (-jnp.arange(nf, dtype=F32) / nf)
    ang = [(t // GRID_W).astype(F32)[:, None] * inv, (t % GRID_W).astype(F32)[:, None] * inv]
    zeros = jnp.zeros((n_pos, nf), F32)

    def lanes(first, second):
        sub = jnp.concatenate([first(ang[0]), second(ang[0]), first(ang[1]), second(ang[1])], axis=1)
        return jnp.concatenate([sub, sub], axis=1)

    cos = lanes(jnp.cos, jnp.cos)
    sin_a = lanes(lambda a: -jnp.sin(a), lambda a: zeros)
    sin_b = lanes(lambda a: zeros, jnp.sin)
    return cos, sin_a, sin_b


def kernel(x_prompt, x_sample, c, cache_k, cache_v, state_C, state_n, state_m, c_ctx, w_mod, b_mod, w_in,
           da_lambda, da_norm_g, ml_conv_w, ml_conv_b, ml_gate_b, ml_norm_g, sg_norm_g, sg_w, sg_b,
           w_branch, w_out, w_up, ffn_conv_w, ffn_conv_b, w_down, final_g):
    B, S, D = x_prompt.shape
    Bd, Sd, _ = x_sample.shape
    L = w_mod.shape[0]
    n_ctx_tok = B * S
    n_streams = 2 * ML_HEADS

    x = jnp.concatenate([x_prompt.reshape(n_ctx_tok, D), x_sample.reshape(Bd * Sd, D)], axis=0)
    cond8 = jnp.concatenate([c_ctx[None, :], c, jnp.zeros((8 - 1 - Bd, D), F32)], axis=0)
    mod = _modulation(cond8, w_mod, b_mod).reshape(L * 8, 1, 6 * D)
    rope_tabs = _rope_tables(Sd)

    wz, wg = _prep_w_in(w_in)
    gate_b = jnp.pad(ml_gate_b.reshape(L, 1, -1), ((0, 0), (0, 0), (0, 128 - N_GATE_COLS)))
    sg_bias = jnp.repeat(jnp.swapaxes(sg_b, 1, 2), BR_W // SG_GROUPS, axis=2)
    sg_w16 = sg_w.astype(BF16)
    w_branch16 = w_branch.astype(BF16)
    w_out16 = w_out.astype(BF16)
    row = lambda p: p.reshape(L, 1, -1)

    caches, states = (), ()
    for l in range(L):
        lam_init = 0.8 - 0.6 * math.exp(-0.3 * l)
        z, gates, kt = _inproj(x, mod, wz, wg, ml_conv_w, row(ml_conv_b), rope_tabs, layer=l,
                               n_ctx_tok=n_ctx_tok, ctx_len=S, lat_len=Sd)

        y_da, *caches = _attn_ctx(z, da_lambda, row(da_norm_g), tuple(caches), layer=l, n_layers=L,
                                  n_seq=B, seq_len=S, lam_init=lam_init)
        y_da = _attn_lat(z, cache_k, cache_v, da_lambda, row(da_norm_g), y_da, layer=l,
                         n_ctx_tok=n_ctx_tok, n_seq=Bd, seq_len=Sd, lam_init=lam_init)

        y_ml, *states = _mlstm(z, kt, gates, gate_b, row(ml_norm_g), None, tuple(states), layer=l,
                               n_layers=L, tok0=0, n_seq=B, seq_len=S, emit_state=True)
        m0 = jnp.broadcast_to(state_m[:, l].reshape(Bd, n_streams, 1), (Bd, n_streams, 128))
        (y_ml,) = _mlstm(z, kt, gates, gate_b, row(ml_norm_g), (state_C, state_n, m0), (y_ml,), layer=l,
                         n_layers=L, tok0=n_ctx_tok, n_seq=Bd, seq_len=Sd, emit_state=False)

        x = _merge(x, mod, y_da, y_ml, z, row(sg_norm_g), sg_w16, sg_bias, w_branch16, w_out16,
                   layer=l, n_ctx_tok=n_ctx_tok, lat_len=Sd)
        x = _ffn(x, mod, w_up, ffn_conv_w, row(ffn_conv_b), w_down, final_g.reshape(1, -1), layer=l,
                 n_ctx_tok=n_ctx_tok, ctx_len=S, lat_len=Sd, final_norm=(l == L - 1))

    y_prompt, y_sample = x
    new_k, new_v = caches
    new_C, new_n, new_m = states
    return (y_prompt.reshape(B, S, D), y_sample.reshape(Bd, Sd, D), new_k, new_v,
            new_C.reshape(B, L, 2, ML_HEADS, ML_HD, ML_HD), new_n.reshape(B, L, 2, ML_HEADS, ML_HD),
            new_m[:, :, :, 0].reshape(B, L, 2, ML_HEADS))
```

```python
import functools
import math

import jax
import jax.numpy as jnp
from jax import lax
from jax.experimental import pallas as pl
from jax.experimental.pallas import tpu as pltpu

F32 = jnp.float32
BF16 = jnp.bfloat16

GRID_W = 64
DA_HEADS = 4
DA_HD = 64
ML_HEADS = 4
ML_HD = 128
CHUNK = 128
SG_GROUPS = 4
BR_W = 512
N_BRANCH = 3
ROPE_THETA = 10000.0
EPS = 1e-6
NEG = -1e30

VMEM_LIMIT_BYTES = 52 * 1024 * 1024

TOKEN_TILE = 1024
COL_TILE = 512
MERGE_TILE = 512
FF_TILE = 256
Q_TILE = 256

ZJ_Q, ZJ_K, ZJ_V, ZJ_MQ, ZJ_MK, ZJ_MV, ZJ_MO, ZJ_SGU = range(8)
ZJ_GATE0 = 8
ZJ_SGV = 14
N_ZJ = 15


def _cparams(sem):
    return pltpu.CompilerParams(dimension_semantics=sem, vmem_limit_bytes=VMEM_LIMIT_BYTES)


def _dot(a, b):
    return jnp.dot(a, b, preferred_element_type=F32)


def _dot_nt(a, b):
    return lax.dot_general(a, b, (((1,), (1,)), ((), ())), preferred_element_type=F32)


def _dot_tn(a, b):
    return lax.dot_general(a, b, (((0,), (0,)), ((), ())), preferred_element_type=F32)


def _mod_kernel(cond_ref, w_ref, b_ref, o_ref):
    a = jax.nn.silu(cond_ref[...]).astype(BF16)
    o_ref[0] = _dot(a, w_ref[0].astype(BF16)) + b_ref[0]


def _modulation(cond8, w_mod, b_mod):
    L, D, N = w_mod.shape
    tn = 768
    return pl.pallas_call(
        _mod_kernel,
        out_shape=jax.ShapeDtypeStruct((L, 8, N), F32),
        grid=(L, N // tn),
        in_specs=[
            pl.BlockSpec((8, D), lambda l, j: (0, 0)),
            pl.BlockSpec((1, D, tn), lambda l, j: (l, 0, j)),
            pl.BlockSpec((1, 1, tn), lambda l, j: (l, 0, j)),
        ],
        out_specs=pl.BlockSpec((1, 8, tn), lambda l, j: (l, 0, j)),
        compiler_params=_cparams(("arbitrary", "arbitrary")),
        name="modulation",
    )(cond8, w_mod, b_mod.reshape(L, 1, N))


N_GATE_COLS = 4 * ML_HEADS
SRC_MLG_TILE = 7


def _wprep_kernel(a_ref, b_ref, wz_ref, wg_ref):
    t = pl.program_id(1)

    @pl.when(t < SRC_MLG_TILE)
    def _():
        wz_ref[0] = a_ref[0].T.astype(BF16)

    @pl.when(t >= SRC_MLG_TILE)
    def _():
        rows = jnp.concatenate([a_ref[0, N_GATE_COLS:, :], b_ref[0]], axis=0)
        wz_ref[0] = rows.T.astype(BF16)

    @pl.when(t == ZJ_SGU)
    def _():
        pad = jnp.zeros((128 - N_GATE_COLS, a_ref.shape[2]), F32)
        wg_ref[0] = jnp.concatenate([a_ref[0, :N_GATE_COLS, :], pad], axis=0).T.astype(BF16)


def _prep_w_in(w_in):
    L, D, n_in = w_in.shape
    assert n_in == N_ZJ * COL_TILE + N_GATE_COLS

    def src_tile(t):
        return jnp.where(t <= ZJ_SGU, t, jnp.where(t == ZJ_SGV, ZJ_SGU + 1, t + 1))

    w_t = jnp.swapaxes(w_in, 1, 2)
    tails_per_tile = COL_TILE // N_GATE_COLS
    return pl.pallas_call(
        _wprep_kernel,
        out_shape=(jax.ShapeDtypeStruct((L, D, N_ZJ * COL_TILE), BF16),
                   jax.ShapeDtypeStruct((L, D, 128), BF16)),
        grid=(L, N_ZJ),
        in_specs=[pl.BlockSpec((1, COL_TILE, D), lambda l, t: (l, src_tile(t), 0)),
                  pl.BlockSpec((1, N_GATE_COLS, D), lambda l, t: (l, (src_tile(t) + 1) * tails_per_tile, 0))],
        out_specs=(pl.BlockSpec((1, D, COL_TILE), lambda l, t: (l, 0, t)),
                   pl.BlockSpec((1, D, 128), lambda l, t: (l, 0, 0))),
        compiler_params=_cparams(("arbitrary", "arbitrary")),
        name="w_in_prep",
    )(w_t, w_t)


def _rms(x):
    return x * lax.rsqrt(jnp.mean(x * x, axis=-1, keepdims=True) + EPS)


def _sigmoid(x):
    return 0.5 * jnp.tanh(0.5 * x) + 0.5


def _silu(x):
    return x * _sigmoid(x)


def _seq_conv3(u, w_ref, b_ref, is_lat, ctx_len):
    rows = u.shape[0]
    prev = pltpu.roll(u, 1, axis=0)
    nxt = pltpu.roll(u, rows - 1, axis=0)
    r8 = lax.broadcasted_iota(jnp.int32, (8, 1), 0)
    interior = jnp.where(is_lat, 1.0, 0.0).astype(F32)
    n_pieces = rows // ctx_len
    pp, nn = [], []
    for p in range(n_pieces):
        lo, hi = p * ctx_len, (p + 1) * ctx_len
        keep_first = interior if p > 0 else 0.0
        keep_last = interior if p < n_pieces - 1 else 0.0
        pp += [prev[lo:lo + 8] * jnp.where(r8 == 0, keep_first, 1.0), prev[lo + 8:hi]]
        nn += [nxt[lo:hi - 8], nxt[hi - 8:hi] * jnp.where(r8 == 7, keep_last, 1.0)]
    prev = jnp.concatenate(pp, axis=0)
    nxt = jnp.concatenate(nn, axis=0)
    return prev * w_ref[0, 0:1, :] + u * w_ref[0, 1:2, :] + nxt * w_ref[0, 2:3, :] + b_ref[0]


def _inproj_kernel(x_ref, mod_ref, wz_ref, wg_ref, cw_ref, cb_ref, cos_ref, sa_ref, sb_ref,
                   z_ref, g_ref, kt_ref, h_scr, *, n_ctx_tiles, ctx_len):
    i = pl.program_id(0)
    j = pl.program_id(1)
    D = x_ref.shape[1]

    @pl.when(j == 0)
    def _():
        h = _rms(x_ref[...]) * (1.0 + mod_ref[0, :, D:2 * D]) + mod_ref[0, :, 0:D]
        hb = h.astype(BF16)
        h_scr[...] = hb
        g_ref[...] = _dot(hb, wg_ref[0])

    z = _dot(h_scr[...], wz_ref[0])
    is_lat = i >= n_ctx_tiles

    def rope(zz):
        parts = []
        for hd in range(zz.shape[1] // 128):
            xh = zz[:, hd * 128:(hd + 1) * 128]
            parts.append(xh * cos_ref[...] + pltpu.roll(xh, 112, axis=1) * sa_ref[...]
                         + pltpu.roll(xh, 16, axis=1) * sb_ref[...])
        return jnp.concatenate(parts, axis=1)

    @pl.when((j == ZJ_Q) & is_lat)
    def _():
        z_ref[...] = (rope(z) * (DA_HD ** -0.5)).astype(BF16)

    @pl.when((j == ZJ_Q) & jnp.logical_not(is_lat))
    def _():
        z_ref[...] = (z * (DA_HD ** -0.5)).astype(BF16)

    @pl.when((j == ZJ_K) & is_lat)
    def _():
        z_ref[...] = rope(z).astype(BF16)

    @pl.when(((j == ZJ_K) & jnp.logical_not(is_lat)) | (j == ZJ_V) | (j == ZJ_MV))
    def _():
        z_ref[...] = z.astype(BF16)

    @pl.when(j == ZJ_MQ)
    def _():
        z_ref[...] = _silu(_seq_conv3(z, cw_ref, cb_ref, is_lat, ctx_len)).astype(BF16)

    @pl.when(j == ZJ_MK)
    def _():
        a = _silu(_seq_conv3(z, cw_ref, cb_ref, is_lat, ctx_len)) * (ML_HD ** -0.5)
        kt_ref[...] = a.T.astype(BF16)

    @pl.when((j == ZJ_MO) | ((j >= ZJ_GATE0) & (j < ZJ_SGV)))
    def _():
        z_ref[...] = _sigmoid(z).astype(BF16)

    @pl.when((j == ZJ_SGU) | (j == ZJ_SGV))
    def _():
        z_ref[...] = jax.nn.gelu(z, approximate=True).astype(BF16)


def _inproj(x, mod, wz, wg, conv_w, conv_b, rope_tabs, *, layer, n_ctx_tok, ctx_len, lat_len):
    n_tok, D = x.shape
    tm, tn = TOKEN_TILE, COL_TILE
    n_ctx_tiles = n_ctx_tok // tm
    assert lat_len == tm and tm % ctx_len == 0 and n_ctx_tok % tm == 0

    def mod_row(i):
        return layer * 8 + jnp.where(i < n_ctx_tiles, 0, i - n_ctx_tiles + 1)

    def conv_col(j):
        return jnp.clip(j - ZJ_MQ, 0, 1)

    kern = functools.partial(_inproj_kernel, n_ctx_tiles=n_ctx_tiles, ctx_len=ctx_len)
    return pl.pallas_call(
        kern,
        out_shape=(jax.ShapeDtypeStruct((n_tok, N_ZJ * tn), BF16),
                   jax.ShapeDtypeStruct((n_tok, 128), F32),
                   jax.ShapeDtypeStruct((tn, n_tok), BF16)),
        grid=(n_tok // tm, N_ZJ),
        in_specs=[
            pl.BlockSpec((tm, D), lambda i, j: (i, 0)),
            pl.BlockSpec((1, 1, 2 * D), lambda i, j: (mod_row(i), 0, 0)),
            pl.BlockSpec((1, D, tn), lambda i, j: (layer, 0, j)),
            pl.BlockSpec((1, D, 128), lambda i, j: (layer, 0, 0)),
            pl.BlockSpec((1, 3, tn), lambda i, j: (layer, 0, conv_col(j))),
            pl.BlockSpec((1, 1, tn), lambda i, j: (layer, 0, conv_col(j))),
            pl.BlockSpec((tm, 128), lambda i, j: (0, 0)),
            pl.BlockSpec((tm, 128), lambda i, j: (0, 0)),
            pl.BlockSpec((tm, 128), lambda i, j: (0, 0)),
        ],
        out_specs=(pl.BlockSpec((tm, tn), lambda i, j: (i, jnp.where(j == ZJ_MK, ZJ_MQ, j))),
                   pl.BlockSpec((tm, 128), lambda i, j: (i, 0)),
                   pl.BlockSpec((tn, tm), lambda i, j: (0, i))),
        scratch_shapes=[pltpu.VMEM((tm, D), BF16)],
        compiler_params=_cparams(("arbitrary", "arbitrary")),
        name="inproj",
    )(x, mod, wz, wg, conv_w, conv_b, *rope_tabs)


def _diff_attn_core(q, ks, vs, lam, gain):
    lane = lax.broadcasted_iota(jnp.int32, (1, 128), 1)
    lo = lane < DA_HD
    zero = jnp.zeros_like(q)
    q1 = jnp.where(lo, q, zero)
    q2 = jnp.where(lo, zero, q)

    def softmax_times_v(qh):
        s = [_dot_nt(qh, k) for k in ks]
        mx = functools.reduce(jnp.maximum, [jnp.max(t, axis=-1, keepdims=True) for t in s])
        e = [jnp.exp(t - mx) for t in s]
        den = functools.reduce(jnp.add, [jnp.sum(t, axis=-1, keepdims=True) for t in e])
        o = functools.reduce(jnp.add, [_dot(t.astype(BF16), v) for t, v in zip(e, vs)])
        return o, 1.0 / den

    o1, r1 = softmax_times_v(q1)
    o2, r2 = softmax_times_v(q2)
    return _rms(o1 * r1 - o2 * (lam * r2)) * gain


def _lambda_full(lam_ref, lam_init):
    p = lam_ref[0]
    s1 = jnp.sum(p[0:1, :] * p[1:2, :], axis=-1, keepdims=True)
    s2 = jnp.sum(p[2:3, :] * p[3:4, :], axis=-1, keepdims=True)
    return jnp.exp(s1) - jnp.exp(s2) + lam_init


def _carried(arrays):
    return [pl.BlockSpec(memory_space=pl.ANY)] * len(arrays)


def _attn_ctx_kernel(*refs, lam_init, n_carried):
    q_ref, k_ref, v_ref, lam_ref, g_ref = refs[:5]
    y_ref, ck_ref, cv_ref = refs[5 + n_carried:]
    lam = _lambda_full(lam_ref, lam_init)
    for h in range(DA_HEADS):
        hs = slice(h * 128, (h + 1) * 128)
        k = k_ref[:, hs]
        v = v_ref[:, hs]
        o = _diff_attn_core(q_ref[:, hs], [k], [v], lam, g_ref[0])
        y_ref[:, hs] = (o * (1.0 - lam_init)).astype(BF16)
        ck_ref[0, 0, h] = k.astype(F32)
        cv_ref[0, 0, h] = v.astype(F32)


def _attn_ctx(z, da_lam, norm_g, carried, *, layer, n_layers, n_seq, seq_len, lam_init):
    n_tok = z.shape[0]
    kern = functools.partial(_attn_ctx_kernel, lam_init=lam_init, n_carried=len(carried))
    blk = lambda zj: pl.BlockSpec((seq_len, COL_TILE), lambda b: (b, zj))
    cache_sds = jax.ShapeDtypeStruct((n_seq, n_layers, DA_HEADS, seq_len, 128), F32)
    cache_blk = pl.BlockSpec((1, 1, DA_HEADS, seq_len, 128), lambda b: (b, layer, 0, 0, 0))
    return pl.pallas_call(
        kern,
        out_shape=(jax.ShapeDtypeStruct((n_tok, BR_W), BF16), cache_sds, cache_sds),
        grid=(n_seq,),
        in_specs=[blk(ZJ_Q), blk(ZJ_K), blk(ZJ_V),
                  pl.BlockSpec((1, 4, DA_HD), lambda b: (layer, 0, 0)),
                  pl.BlockSpec((1, 1, 128), lambda b: (layer, 0, 0))] + _carried(carried),
        out_specs=(pl.BlockSpec((seq_len, BR_W), lambda b: (b, 0)), cache_blk, cache_blk),
        input_output_aliases={5 + n: 1 + n for n in range(len(carried))},
        compiler_params=_cparams(("arbitrary",)),
        name="attn_ctx",
    )(z, z, z, da_lam, norm_g, *carried)


def _attn_lat_kernel(q_ref, k_ref, v_ref, ck_ref, cv_ref, lam_ref, g_ref, yin_ref, y_ref, *, lam_init):
    del yin_ref
    lam = _lambda_full(lam_ref, lam_init)
    ks = [ck_ref[0, 0, 0].astype(BF16), k_ref[...]]
    vs = [cv_ref[0, 0, 0].astype(BF16), v_ref[...]]
    for t in range(q_ref.shape[0] // Q_TILE):
        rows = slice(t * Q_TILE, (t + 1) * Q_TILE)
        o = _diff_attn_core(q_ref[rows, :], ks, vs, lam, g_ref[0])
        y_ref[rows, :] = (o * (1.0 - lam_init)).astype(BF16)


def _attn_lat(z, cache_k, cache_v, da_lam, norm_g, y_da, *, layer, n_ctx_tok, n_seq, seq_len, lam_init):
    hb = COL_TILE // 128
    s0 = n_ctx_tok // seq_len
    past = cache_k.shape[3]
    kern = functools.partial(_attn_lat_kernel, lam_init=lam_init)
    zblk = lambda zj: pl.BlockSpec((seq_len, 128), lambda b, h: (s0 + b, zj * hb + h))
    return pl.pallas_call(
        kern,
        out_shape=jax.ShapeDtypeStruct(y_da.shape, y_da.dtype),
        grid=(n_seq, DA_HEADS),
        in_specs=[
            zblk(ZJ_Q), zblk(ZJ_K), zblk(ZJ_V),
            pl.BlockSpec((1, 1, 1, past, 128), lambda b, h: (b, layer, h, 0, 0)),
            pl.BlockSpec((1, 1, 1, past, 128), lambda b, h: (b, layer, h, 0, 0)),
            pl.BlockSpec((1, 4, DA_HD), lambda b, h: (layer, 0, 0)),
            pl.BlockSpec((1, 1, 128), lambda b, h: (layer, 0, 0)),
            pl.BlockSpec(memory_space=pl.ANY),
        ],
        out_specs=pl.BlockSpec((seq_len, 128), lambda b, h: (s0 + b, h)),
        input_output_aliases={7: 0},
        compiler_params=_cparams(("arbitrary", "arbitrary")),
        name="attn_lat",
    )(z, z, z, cache_k, cache_v, da_lam, norm_g, y_da)


def _split3(x):
    hi = x.astype(BF16)
    r = x - hi.astype(F32)
    mid = r.astype(BF16)
    lo = (r - mid.astype(F32)).astype(BF16)
    return hi, mid, lo


def _mlstm_kernel(*refs, n_chunks, has_init, emit_state, n_carried):
    q_ref, kt_ref, v_ref, og_ref, gt_ref, gb_ref, ng_ref = refs[:7]
    pos = 7
    if has_init:
        c0_ref, n0_ref, m0_ref = refs[pos:pos + 3]
        pos += 3
    pos += n_carried
    y_ref = refs[pos]
    pos += 1
    if emit_state:
        co_ref, no_ref, mo_ref = refs[pos:pos + 3]
        pos += 3
    h_fw, h_bw, cn_s, m_s = refs[pos:pos + 4]
    hdir = (h_fw, h_bw)

    n_streams = 2 * ML_HEADS
    for s in range(n_streams):
        d, h = divmod(s, ML_HEADS)
        if has_init:
            n_col = jnp.broadcast_to(n0_ref[0, 0, d, h:h + 1, :], (ML_HD, ML_HD)).T
            cn_s[s] = jnp.concatenate([c0_ref[0, 0, d, h], n_col], axis=1)
            m_s[s] = m0_ref[0, s:s + 1, :]
        else:
            cn_s[s] = jnp.zeros((ML_HD, 2 * ML_HD), F32)
            m_s[s] = jnp.zeros((1, 128), F32)

    row = lax.broadcasted_iota(jnp.int32, (CHUNK, CHUNK), 0)
    col = lax.broadcasted_iota(jnp.int32, (CHUNK, CHUNK), 1)
    masks = (col <= row, col >= row)
    tris = tuple(jnp.where(m, 1.0, 0.0).astype(BF16) for m in masks)
    last_row = (CHUNK - 1, 0)

    ones = jnp.ones((CHUNK, ML_HD), BF16)

    def step(c_fw):
        rows_d, prep = [], []
        for d in range(2):
            c = c_fw if d == 0 else n_chunks - 1 - c_fw
            r0 = c * CHUNK
            if not isinstance(r0, int):
                r0 = pl.multiple_of(r0, CHUNK)
            rows = pl.ds(r0, CHUNK)
            pre = gt_ref[rows, :] + gb_ref[0]
            logf = jax.nn.log_sigmoid(pre)
            hi, mid, lo = _split3(logf)
            csum = _dot(tris[d], hi) + _dot(tris[d], mid) + _dot(tris[d], lo)
            rows_d.append(rows)
            prep.append((csum, pre.T, csum.T))

        streams = [(d, h) for d in range(2) for h in range(ML_HEADS)]
        st = []
        for d, h in streams:
            s = d * ML_HEADS + h
            csum, pre_t, csum_t = prep[d]
            rows = rows_d[d]
            li, lf = d * ML_HEADS + h, 2 * ML_HEADS + d * ML_HEADS + h
            hs = slice(h * ML_HD, (h + 1) * ML_HD)
            b_t = jnp.broadcast_to(csum[:, lf:lf + 1], (CHUNK, 128))
            r_row = pre_t[li:li + 1, :] - csum_t[lf:lf + 1, :]
            m_prev = m_s[s]
            qc = q_ref[rows, hs]
            ktc = kt_ref[hs, rows]
            log_w = jnp.where(masks[d], b_t + r_row, NEG)
            inter = b_t + m_prev
            m_t = jnp.maximum(inter, jnp.broadcast_to(jnp.max(log_w, axis=-1, keepdims=True),
                                                      (CHUNK, 128)))
            st.append(dict(s=s, d=d, rows=rows, hs=hs, b_t=b_t, r_row=r_row, m_prev=m_prev, qc=qc,
                           ktc=ktc, log_w=log_w, inter=inter, m_t=m_t, qkt=_dot(qc, ktc),
                           qcn=_dot(qc, cn_s[s].astype(BF16))))

        for e in st:
            e["v_ext"] = jnp.concatenate([v_ref[e["rows"], e["hs"]], ones], axis=1)
            m_t = e["m_t"]
            qk = (e["qkt"] * jnp.exp(e["log_w"] - m_t)).astype(BF16)
            s_inter = jnp.exp(e["inter"] - m_t)
            both = _dot(qk, e["v_ext"]) + jnp.concatenate([s_inter, s_inter], axis=1) * e["qcn"]
            num, den = both[:, :ML_HD], both[:, ML_HD:]
            hdir[e["d"]][e["rows"], e["hs"]] = num / jnp.maximum(jnp.abs(den), jnp.exp(-m_t))

        for e in st:
            s, lr = e["s"], last_row[e["d"]]
            m_new = e["m_t"][lr:lr + 1, :]
            b_last = e["b_t"][lr:lr + 1, :]
            g_row = jnp.exp(b_last + e["r_row"] - m_new)
            decay = jnp.exp(b_last + e["m_prev"] - m_new)
            gkt = (e["ktc"].astype(F32) * g_row).astype(BF16)
            cn_s[s] = jnp.concatenate([decay, decay], axis=1) * cn_s[s] + _dot(gkt, e["v_ext"])
            m_s[s] = m_new

    if n_chunks <= 2:
        for c in range(n_chunks):
            step(c)
    else:
        def body(c, carry):
            step(c)
            return carry
        lax.fori_loop(0, n_chunks, body, 0)

    for c in range(n_chunks):
        rows = slice(c * CHUNK, (c + 1) * CHUNK)
        for h in range(ML_HEADS):
            hs = slice(h * ML_HD, (h + 1) * ML_HD)
            hn = _rms(h_fw[rows, hs] + h_bw[rows, hs]) * ng_ref[0]
            y_ref[rows, hs] = (og_ref[rows, hs].astype(F32) * hn.astype(F32)).astype(BF16)

    if emit_state:
        for s in range(n_streams):
            cn = cn_s[s]
            co_ref[0, 0, s] = cn[:, :ML_HD]
            no_ref[0, 0, s:s + 1, :] = cn[:, ML_HD:].T[0:1, :]
            mo_ref[0, 0, s:s + 1, :] = m_s[s]


def _mlstm(z, kt, gates, gate_b, norm_g, init, carried, *, layer, n_layers, tok0, n_seq, seq_len, emit_state):
    n_tok = z.shape[0]
    s0 = tok0 // seq_len
    n_chunks = seq_len // CHUNK
    has_init = init is not None
    n_streams = 2 * ML_HEADS
    zblk = lambda zj: pl.BlockSpec((seq_len, BR_W), lambda b: (s0 + b, zj))
    in_specs = [zblk(ZJ_MQ), pl.BlockSpec((BR_W, seq_len), lambda b: (0, s0 + b)), zblk(ZJ_MV), zblk(ZJ_MO),
                pl.BlockSpec((seq_len, 128), lambda b: (s0 + b, 0)),
                pl.BlockSpec((1, 1, 128), lambda b: (layer, 0, 0)),
                pl.BlockSpec((1, 1, 128), lambda b: (layer, 0, 0))]
    args = [z, kt, z, z, gates, gate_b, norm_g]
    if has_init:
        c0, n0, m0 = init
        in_specs += [
            pl.BlockSpec((1, 1, 2, ML_HEADS, ML_HD, ML_HD), lambda b: (b, layer, 0, 0, 0, 0)),
            pl.BlockSpec((1, 1, 2, ML_HEADS, ML_HD), lambda b: (b, layer, 0, 0, 0)),
            pl.BlockSpec((1, n_streams, 128), lambda b: (b, 0, 0)),
        ]
        args += [c0, n0, m0]
    first_carried = len(args)
    in_specs += _carried(carried)
    args += list(carried)
    out_shape = [jax.ShapeDtypeStruct((n_tok, BR_W), BF16)]
    out_specs = [pl.BlockSpec((seq_len, BR_W), lambda b: (s0 + b, 0))]
    if emit_state:
        out_shape += [jax.ShapeDtypeStruct((n_seq, n_layers, n_streams, ML_HD, ML_HD), F32),
                      jax.ShapeDtypeStruct((n_seq, n_layers, n_streams, ML_HD), F32),
                      jax.ShapeDtypeStruct((n_seq, n_layers, n_streams, 128), F32)]
        out_specs += [pl.BlockSpec((1, 1, n_streams, ML_HD, ML_HD), lambda b: (b, layer, 0, 0, 0)),
                      pl.BlockSpec((1, 1, n_streams, ML_HD), lambda b: (b, layer, 0, 0)),
                      pl.BlockSpec((1, 1, n_streams, 128), lambda b: (b, layer, 0, 0))]
        aliases = {first_carried + n: 1 + n for n in range(len(carried))}
    else:
        aliases = {first_carried: 0}
    kern = functools.partial(_mlstm_kernel, n_chunks=n_chunks, has_init=has_init, emit_state=emit_state,
                             n_carried=len(carried))
    return pl.pallas_call(
        kern,
        out_shape=tuple(out_shape),
        grid=(n_seq,),
        in_specs=in_specs,
        out_specs=tuple(out_specs),
        input_output_aliases=aliases,
        scratch_shapes=[pltpu.VMEM((seq_len, BR_W), F32), pltpu.VMEM((seq_len, BR_W), F32),
                        pltpu.VMEM((n_streams, ML_HD, 2 * ML_HD), F32),
                        pltpu.VMEM((n_streams, 1, 128), F32)],
        compiler_params=_cparams(("arbitrary",)),
        name="mlstm_init" if has_init else "mlstm_zero",
    )(*args)


def _merge_kernel(x_ref, mod_ref, yda_ref, yml_ref, sgu_ref, sgv_ref, g0_ref, g1_ref, g2_ref,
                  sgn_ref, sgw_ref, sgb_ref, wb_ref, wo_ref, o_ref):
    D = x_ref.shape[1]
    tm = x_ref.shape[0]
    v = sgv_ref[...].astype(F32)
    vc = v - jnp.mean(v, axis=-1, keepdims=True)
    sv = (vc * lax.rsqrt(jnp.mean(vc * vc, axis=-1, keepdims=True) + EPS) * sgn_ref[0]).astype(BF16)
    mixed = []
    for c in range(tm // CHUNK):
        rows = slice(c * CHUNK, (c + 1) * CHUNK)
        groups = [_dot(sgw_ref[0, g], sv[rows, g * 128:(g + 1) * 128]) for g in range(SG_GROUPS)]
        mixed.append(jnp.concatenate(groups, axis=1) + sgb_ref[0])
    y_sg = (sgu_ref[...].astype(F32) * jnp.concatenate(mixed, axis=0)).astype(BF16)

    m = g0_ref[...].astype(F32) * _dot(yda_ref[...], wb_ref[0, 0])
    m = m + g1_ref[...].astype(F32) * _dot(yml_ref[...], wb_ref[0, 1])
    m = m + g2_ref[...].astype(F32) * _dot(y_sg, wb_ref[0, 2])
    out = _dot(m.astype(BF16), wo_ref[0])
    o_ref[...] = x_ref[...] + mod_ref[0, :, 2 * D:3 * D] * out


def _mod_row_fn(layer, tile, n_ctx_tok, lat_len):
    def f(i):
        t0 = i * tile
        return layer * 8 + jnp.where(t0 < n_ctx_tok, 0, 1 + (t0 - n_ctx_tok) // lat_len)
    return f


def _merge(x, mod, y_da, y_ml, z, sg_norm_g, sg_w, sg_bias, w_branch, w_out, *, layer, n_ctx_tok, lat_len):
    n_tok, D = x.shape
    tm = MERGE_TILE
    mrow = _mod_row_fn(layer, tm, n_ctx_tok, lat_len)
    gate_blk = lambda n: pl.BlockSpec((tm, D), lambda i: (i, ZJ_GATE0 * COL_TILE // D + n))
    return pl.pallas_call(
        _merge_kernel,
        out_shape=jax.ShapeDtypeStruct((n_tok, D), F32),
        grid=(n_tok // tm,),
        in_specs=[
            pl.BlockSpec((tm, D), lambda i: (i, 0)),
            pl.BlockSpec((1, 1, 6 * D), lambda i: (mrow(i), 0, 0)),
            pl.BlockSpec((tm, BR_W), lambda i: (i, 0)),
            pl.BlockSpec((tm, BR_W), lambda i: (i, 0)),
            pl.BlockSpec((tm, COL_TILE), lambda i: (i, ZJ_SGU)),
            pl.BlockSpec((tm, COL_TILE), lambda i: (i, ZJ_SGV)),
            gate_blk(0), gate_blk(1), gate_blk(2),
            pl.BlockSpec((1, 1, BR_W), lambda i: (layer, 0, 0)),
            pl.BlockSpec((1, SG_GROUPS, CHUNK, CHUNK), lambda i: (layer, 0, 0, 0)),
            pl.BlockSpec((1, CHUNK, BR_W), lambda i: (layer, 0, 0)),
            pl.BlockSpec((1, N_BRANCH, BR_W, D), lambda i: (layer, 0, 0, 0)),
            pl.BlockSpec((1, D, D), lambda i: (layer, 0, 0)),
        ],
        out_specs=pl.BlockSpec((tm, D), lambda i: (i, 0)),
        compiler_params=_cparams(("arbitrary",)),
        name="merge",
    )(x, mod, y_da, y_ml, z, z, z, z, z, sg_norm_g, sg_w, sg_bias, w_branch, w_out)


def _ffn_kernel(x_ref, mod_ref, wa_ref, wg_ref, cwa_ref, cwg_ref, cba_ref, cbg_ref, wd_ref, fg_ref,
                *rest, n_ctx_tiles, ctx_len, n_chunks, final_norm):
    out_refs, (h_scr, acc, u0, u1) = rest[:-4], rest[-4:]
    ubufs = (u0, u1)
    i = pl.program_id(0)
    c = pl.program_id(1)
    D = x_ref.shape[1]
    is_lat = i >= n_ctx_tiles

    def run(par, up, down):
        if up:
            h = h_scr[...]
            ubufs[par][0] = _dot(h, wa_ref[0].astype(BF16))
            ubufs[par][1] = _dot(h, wg_ref[0].astype(BF16))
        if down:
            ua = _seq_conv3(ubufs[1 - par][0], cwa_ref, cba_ref, is_lat, ctx_len)
            ug = _seq_conv3(ubufs[1 - par][1], cwg_ref, cbg_ref, is_lat, ctx_len)
            a = (_silu(ua) * ug).astype(BF16)
            acc[...] += _dot(a, wd_ref[0].astype(BF16))

    @pl.when(c == 0)
    def _():
        h = _rms(x_ref[...]) * (1.0 + mod_ref[0, :, 4 * D:5 * D]) + mod_ref[0, :, 3 * D:4 * D]
        h_scr[...] = h.astype(BF16)
        acc[...] = jnp.zeros_like(acc)
        run(0, True, False)

    steady = (c >= 1) & (c < n_chunks)
    for par in range(2):
        @pl.when(steady & (c % 2 == par))
        def _():
            run(par, True, True)

    last = c == n_chunks

    @pl.when(last)
    def _():
        run(n_chunks % 2, False, True)

    if final_norm:
        yp_ref, ys_ref = out_refs

        def result():
            return _rms(x_ref[...] + mod_ref[0, :, 5 * D:6 * D] * acc[...]) * fg_ref[...]

        @pl.when(last & jnp.logical_not(is_lat))
        def _():
            yp_ref[...] = result()

        @pl.when(last & is_lat)
        def _():
            ys_ref[...] = result()
    else:
        @pl.when(last)
        def _():
            out_refs[0][...] = x_ref[...] + mod_ref[0, :, 5 * D:6 * D] * acc[...]


def _ffn(x, mod, w_up, conv_w, conv_b, w_down, final_g, *, layer, n_ctx_tok, ctx_len, lat_len, final_norm):
    n_tok, D = x.shape
    d_ff = w_down.shape[1]
    tm, tc = TOKEN_TILE, FF_TILE
    nc = d_ff // tc
    n_ctx_tiles = n_ctx_tok // tm
    mrow = _mod_row_fn(layer, tm, n_ctx_tok, lat_len)
    assert nc >= 2 and lat_len == tm
    kern = functools.partial(_ffn_kernel, n_ctx_tiles=n_ctx_tiles, ctx_len=ctx_len, n_chunks=nc,
                             final_norm=final_norm)
    up_c = lambda c: jnp.minimum(c, nc - 1)
    down_c = lambda c: jnp.maximum(c - 1, 0)
    if final_norm:
        out_shape = (jax.ShapeDtypeStruct((n_ctx_tok, D), F32), jax.ShapeDtypeStruct((n_tok - n_ctx_tok, D), F32))
        out_specs = (pl.BlockSpec((tm, D), lambda i, c: (jnp.minimum(i, n_ctx_tiles - 1), 0)),
                     pl.BlockSpec((tm, D), lambda i, c: (jnp.maximum(i - n_ctx_tiles, 0), 0)))
    else:
        out_shape = jax.ShapeDtypeStruct((n_tok, D), F32)
        out_specs = pl.BlockSpec((tm, D), lambda i, c: (i, 0))
    return pl.pallas_call(
        kern,
        out_shape=out_shape,
        grid=(n_tok // tm, nc + 1),
        in_specs=[
            pl.BlockSpec((tm, D), lambda i, c: (i, 0)),
            pl.BlockSpec((1, 1, 6 * D), lambda i, c: (mrow(i), 0, 0)),
            pl.BlockSpec((1, D, tc), lambda i, c: (layer, 0, up_c(c))),
            pl.BlockSpec((1, D, tc), lambda i, c: (layer, 0, nc + up_c(c))),
            pl.BlockSpec((1, 3, tc), lambda i, c: (layer, 0, down_c(c))),
            pl.BlockSpec((1, 3, tc), lambda i, c: (layer, 0, nc + down_c(c))),
            pl.BlockSpec((1, 1, tc), lambda i, c: (layer, 0, down_c(c))),
            pl.BlockSpec((1, 1, tc), lambda i, c: (layer, 0, nc + down_c(c))),
            pl.BlockSpec((1, tc, D), lambda i, c: (layer, down_c(c), 0)),
            pl.BlockSpec((1, D), lambda i, c: (0, 0)),
        ],
        out_specs=out_specs,
        scratch_shapes=[pltpu.VMEM((tm, D), BF16), pltpu.VMEM((tm, D), F32),
                        pltpu.VMEM((2, tm, tc), F32), pltpu.VMEM((2, tm, tc), F32)],
        compiler_params=_cparams(("arbitrary", "arbitrary")),
        name="ffn_final" if final_norm else "ffn",
    )(x, mod, w_up, w_up, conv_w, conv_w, conv_b, conv_b, w_down, final_g)


def _rope_tables(n_pos):
    t = jnp.arange(n_pos)
    nf = DA_HD // 4
    inv = ROPE_THETA ** (-jnp.arange(nf, dtype=F32) / nf)
    ang = [(t // GRID_W).astype(F32)[:, None] * inv, (t % GRID_W).astype(F32)[:, None] * inv]
    zeros = jnp.zeros((n_pos, nf), F32)

    def lanes(first, second):
        sub = jnp.concatenate([first(ang[0]), second(ang[0]), first(ang[1]), second(ang[1])], axis=1)
        return jnp.concatenate([sub, sub], axis=1)

    cos = lanes(jnp.cos, jnp.cos)
    sin_a = lanes(lambda a: -jnp.sin(a), lambda a: zeros)
    sin_b = lanes(lambda a: zeros, jnp.sin)
    return cos, sin_a, sin_b


def kernel(x_prompt, x_sample, c, cache_k, cache_v, state_C, state_n, state_m, c_ctx, w_mod, b_mod, w_in,
           da_lambda, da_norm_g, ml_conv_w, ml_conv_b, ml_gate_b, ml_norm_g, sg_norm_g, sg_w, sg_b,
           w_branch, w_out, w_up, ffn_conv_w, ffn_conv_b, w_down, final_g):
    B, S, D = x_prompt.shape
    Bd, Sd, _ = x_sample.shape
    L = w_mod.shape[0]
    n_ctx_tok = B * S
    n_streams = 2 * ML_HEADS

    x = jnp.concatenate([x_prompt.reshape(n_ctx_tok, D), x_sample.reshape(Bd * Sd, D)], axis=0)
    cond8 = jnp.concatenate([c_ctx[None, :], c, jnp.zeros((8 - 1 - Bd, D), F32)], axis=0)
    mod = _modulation(cond8, w_mod, b_mod).reshape(L * 8, 1, 6 * D)
    rope_tabs = _rope_tables(Sd)

    wz, wg = _prep_w_in(w_in)
    gate_b = jnp.pad(ml_gate_b.reshape(L, 1, -1), ((0, 0), (0, 0), (0, 128 - N_GATE_COLS)))
    sg_bias = jnp.repeat(jnp.swapaxes(sg_b, 1, 2), BR_W // SG_GROUPS, axis=2)
    sg_w16 = sg_w.astype(BF16)
    w_branch16 = w_branch.astype(BF16)
    w_out16 = w_out.astype(BF16)
    row = lambda p: p.reshape(L, 1, -1)

    caches, states = (), ()
    for l in range(L):
        lam_init = 0.8 - 0.6 * math.exp(-0.3 * l)
        z, gates, kt = _inproj(x, mod, wz, wg, ml_conv_w, row(ml_conv_b), rope_tabs, layer=l,
                               n_ctx_tok=n_ctx_tok, ctx_len=S, lat_len=Sd)

        y_da, *caches = _attn_ctx(z, da_lambda, row(da_norm_g), tuple(caches), layer=l, n_layers=L,
                                  n_seq=B, seq_len=S, lam_init=lam_init)
        y_da = _attn_lat(z, cache_k, cache_v, da_lambda, row(da_norm_g), y_da, layer=l,
                         n_ctx_tok=n_ctx_tok, n_seq=Bd, seq_len=Sd, lam_init=lam_init)

        y_ml, *states = _mlstm(z, kt, gates, gate_b, row(ml_norm_g), None, tuple(states), layer=l,
                               n_layers=L, tok0=0, n_seq=B, seq_len=S, emit_state=True)
        m0 = jnp.broadcast_to(state_m[:, l].reshape(Bd, n_streams, 1), (Bd, n_streams, 128))
        (y_ml,) = _mlstm(z, kt, gates, gate_b, row(ml_norm_g), (state_C, state_n, m0), (y_ml,), layer=l,
                         n_layers=L, tok0=n_ctx_tok, n_seq=Bd, seq_len=Sd, emit_state=False)

        x = _merge(x, mod, y_da, y_ml, z, row(sg_norm_g), sg_w16, sg_bias, w_branch16, w_out16,
                   layer=l, n_ctx_tok=n_ctx_tok, lat_len=Sd)
        x = _ffn(x, mod, w_up, ffn_conv_w, row(ffn_conv_b), w_down, final_g.reshape(1, -1), layer=l,
                 n_ctx_tok=n_ctx_tok, ctx_len=S, lat_len=Sd, final_norm=(l == L - 1))

    y_prompt, y_sample = x
    new_k, new_v = caches
    new_C, new_n, new_m = states
    return (y_prompt.reshape(B, S, D), y_sample.reshape(Bd, Sd, D), new_k, new_v,
            new_C.reshape(B, L, 2, ML_HEADS, ML_HD, ML_HD), new_n.reshape(B, L, 2, ML_HEADS, ML_HD),
            new_m[:, :, :, 0].reshape(B, L, 2, ML_HEADS))
```

```python
import functools
import math

import jax
import jax.numpy as jnp
from jax import lax
from jax.experimental import pallas as pl
from jax.experimental.pallas import tpu as pltpu

F32 = jnp.float32
BF16 = jnp.bfloat16

GRID_W = 64
DA_HEADS = 4
DA_HD = 64
ML_HEADS = 4
ML_HD = 128
CHUNK = 128
SG_GROUPS = 4
BR_W = 512
N_BRANCH = 3
ROPE_THETA = 10000.0
EPS = 1e-6
NEG = -1e30

VMEM_LIMIT_BYTES = 52 * 1024 * 1024

TOKEN_TILE = 1024
COL_TILE = 512
MERGE_TILE = 1024
FF_TILE = 256
Q_TILE = 256

ZJ_Q, ZJ_K, ZJ_V, ZJ_MQ, ZJ_MK, ZJ_MV, ZJ_MO, ZJ_SGU = range(8)
ZJ_GATE0 = 8
ZJ_SGV = 14
N_ZJ = 15


def _cparams(sem):
    return pltpu.CompilerParams(dimension_semantics=sem, vmem_limit_bytes=VMEM_LIMIT_BYTES)


def _dot(a, b):
    return jnp.dot(a, b, preferred_element_type=F32)


def _dot_nt(a, b):
    return lax.dot_general(a, b, (((1,), (1,)), ((), ())), preferred_element_type=F32)


def _dot_tn(a, b):
    return lax.dot_general(a, b, (((0,), (0,)), ((), ())), preferred_element_type=F32)


def _mod_kernel(cond_ref, w_ref, b_ref, o_ref):
    a = jax.nn.silu(cond_ref[...]).astype(BF16)
    o_ref[0] = _dot(a, w_ref[0].astype(BF16)) + b_ref[0]


def _modulation(cond8, w_mod, b_mod):
    L, D, N = w_mod.shape
    tn = 768
    return pl.pallas_call(
        _mod_kernel,
        out_shape=jax.ShapeDtypeStruct((L, 8, N), F32),
        grid=(L, N // tn),
        in_specs=[
            pl.BlockSpec((8, D), lambda l, j: (0, 0)),
            pl.BlockSpec((1, D, tn), lambda l, j: (l, 0, j)),
            pl.BlockSpec((1, 1, tn), lambda l, j: (l, 0, j)),
        ],
        out_specs=pl.BlockSpec((1, 8, tn), lambda l, j: (l, 0, j)),
        compiler_params=_cparams(("arbitrary", "arbitrary")),
        name="modulation",
    )(cond8, w_mod, b_mod.reshape(L, 1, N))


N_GATE_COLS = 4 * ML_HEADS
SRC_MLG_TILE = 7


def _wprep_kernel(a_ref, b_ref, wz_ref, wg_ref):
    t = pl.program_id(1)

    @pl.when(t < SRC_MLG_TILE)
    def _():
        wz_ref[0] = a_ref[0].T.astype(BF16)

    @pl.when(t >= SRC_MLG_TILE)
    def _():
        rows = jnp.concatenate([a_ref[0, N_GATE_COLS:, :], b_ref[0]], axis=0)
        wz_ref[0] = rows.T.astype(BF16)

    @pl.when(t == ZJ_SGU)
    def _():
        pad = jnp.zeros((128 - N_GATE_COLS, a_ref.shape[2]), F32)
        wg_ref[0] = jnp.concatenate([a_ref[0, :N_GATE_COLS, :], pad], axis=0).T.astype(BF16)


def _prep_w_in(w_in):
    L, D, n_in = w_in.shape
    assert n_in == N_ZJ * COL_TILE + N_GATE_COLS

    def src_tile(t):
        return jnp.where(t <= ZJ_SGU, t, jnp.where(t == ZJ_SGV, ZJ_SGU + 1, t + 1))

    w_t = jnp.swapaxes(w_in, 1, 2)
    tails_per_tile = COL_TILE // N_GATE_COLS
    return pl.pallas_call(
        _wprep_kernel,
        out_shape=(jax.ShapeDtypeStruct((L, D, N_ZJ * COL_TILE), BF16),
                   jax.ShapeDtypeStruct((L, D, 128), BF16)),
        grid=(L, N_ZJ),
        in_specs=[pl.BlockSpec((1, COL_TILE, D), lambda l, t: (l, src_tile(t), 0)),
                  pl.BlockSpec((1, N_GATE_COLS, D), lambda l, t: (l, (src_tile(t) + 1) * tails_per_tile, 0))],
        out_specs=(pl.BlockSpec((1, D, COL_TILE), lambda l, t: (l, 0, t)),
                   pl.BlockSpec((1, D, 128), lambda l, t: (l, 0, 0))),
        compiler_params=_cparams(("arbitrary", "arbitrary")),
        name="w_in_prep",
    )(w_t, w_t)


def _rms(x):
    return x * lax.rsqrt(jnp.mean(x * x, axis=-1, keepdims=True) + EPS)


def _sigmoid(x):
    return 0.5 * jnp.tanh(0.5 * x) + 0.5


def _silu(x):
    return x * _sigmoid(x)


def _seq_conv3(u, w_ref, b_ref, is_lat, ctx_len):
    rows = u.shape[0]
    prev = pltpu.roll(u, 1, axis=0)
    nxt = pltpu.roll(u, rows - 1, axis=0)
    r8 = lax.broadcasted_iota(jnp.int32, (8, 1), 0)
    interior = jnp.where(is_lat, 1.0, 0.0).astype(F32)
    n_pieces = rows // ctx_len
    pp, nn = [], []
    for p in range(n_pieces):
        lo, hi = p * ctx_len, (p + 1) * ctx_len
        keep_first = interior if p > 0 else 0.0
        keep_last = interior if p < n_pieces - 1 else 0.0
        pp += [prev[lo:lo + 8] * jnp.where(r8 == 0, keep_first, 1.0), prev[lo + 8:hi]]
        nn += [nxt[lo:hi - 8], nxt[hi - 8:hi] * jnp.where(r8 == 7, keep_last, 1.0)]
    prev = jnp.concatenate(pp, axis=0)
    nxt = jnp.concatenate(nn, axis=0)
    return prev * w_ref[0, 0:1, :] + u * w_ref[0, 1:2, :] + nxt * w_ref[0, 2:3, :] + b_ref[0]


def _inproj_kernel(x_ref, mod_ref, wz_ref, wg_ref, cw_ref, cb_ref, cos_ref, sa_ref, sb_ref,
                   z_ref, g_ref, kt_ref, h_scr, *, n_ctx_tiles, ctx_len):
    i = pl.program_id(0)
    j = pl.program_id(1)
    D = x_ref.shape[1]

    @pl.when(j == 0)
    def _():
        h = _rms(x_ref[...]) * (1.0 + mod_ref[0, :, D:2 * D]) + mod_ref[0, :, 0:D]
        hb = h.astype(BF16)
        h_scr[...] = hb
        g_ref[...] = _dot(hb, wg_ref[0])

    z = _dot(h_scr[...], wz_ref[0])
    is_lat = i >= n_ctx_tiles

    def rope(zz):
        parts = []
        for hd in range(zz.shape[1] // 128):
            xh = zz[:, hd * 128:(hd + 1) * 128]
            parts.append(xh * cos_ref[...] + pltpu.roll(xh, 112, axis=1) * sa_ref[...]
                         + pltpu.roll(xh, 16, axis=1) * sb_ref[...])
        return jnp.concatenate(parts, axis=1)

    @pl.when((j == ZJ_Q) & is_lat)
    def _():
        z_ref[...] = (rope(z) * (DA_HD ** -0.5)).astype(BF16)

    @pl.when((j == ZJ_Q) & jnp.logical_not(is_lat))
    def _():
        z_ref[...] = (z * (DA_HD ** -0.5)).astype(BF16)

    @pl.when((j == ZJ_K) & is_lat)
    def _():
        z_ref[...] = rope(z).astype(BF16)

    @pl.when(((j == ZJ_K) & jnp.logical_not(is_lat)) | (j == ZJ_V) | (j == ZJ_MV))
    def _():
        z_ref[...] = z.astype(BF16)

    @pl.when(j == ZJ_MQ)
    def _():
        z_ref[...] = _silu(_seq_conv3(z, cw_ref, cb_ref, is_lat, ctx_len)).astype(BF16)

    @pl.when(j == ZJ_MK)
    def _():
        a = _silu(_seq_conv3(z, cw_ref, cb_ref, is_lat, ctx_len)) * (ML_HD ** -0.5)
        kt_ref[...] = a.T.astype(BF16)

    @pl.when((j == ZJ_MO) | ((j >= ZJ_GATE0) & (j < ZJ_SGV)))
    def _():
        z_ref[...] = _sigmoid(z).astype(BF16)

    @pl.when((j == ZJ_SGU) | (j == ZJ_SGV))
    def _():
        z_ref[...] = jax.nn.gelu(z, approximate=True).astype(BF16)


def _inproj(x, mod, wz, wg, conv_w, conv_b, rope_tabs, *, layer, n_ctx_tok, ctx_len, lat_len):
    n_tok, D = x.shape
    tm, tn = TOKEN_TILE, COL_TILE
    n_ctx_tiles = n_ctx_tok // tm
    assert lat_len == tm and tm % ctx_len == 0 and n_ctx_tok % tm == 0

    def mod_row(i):
        return layer * 8 + jnp.where(i < n_ctx_tiles, 0, i - n_ctx_tiles + 1)

    def conv_col(j):
        return jnp.clip(j - ZJ_MQ, 0, 1)

    kern = functools.partial(_inproj_kernel, n_ctx_tiles=n_ctx_tiles, ctx_len=ctx_len)
    return pl.pallas_call(
        kern,
        out_shape=(jax.ShapeDtypeStruct((n_tok, N_ZJ * tn), BF16),
                   jax.ShapeDtypeStruct((n_tok, 128), F32),
                   jax.ShapeDtypeStruct((tn, n_tok), BF16)),
        grid=(n_tok // tm, N_ZJ),
        in_specs=[
            pl.BlockSpec((tm, D), lambda i, j: (i, 0)),
            pl.BlockSpec((1, 1, 2 * D), lambda i, j: (mod_row(i), 0, 0)),
            pl.BlockSpec((1, D, tn), lambda i, j: (layer, 0, j)),
            pl.BlockSpec((1, D, 128), lambda i, j: (layer, 0, 0)),
            pl.BlockSpec((1, 3, tn), lambda i, j: (layer, 0, conv_col(j))),
            pl.BlockSpec((1, 1, tn), lambda i, j: (layer, 0, conv_col(j))),
            pl.BlockSpec((tm, 128), lambda i, j: (0, 0)),
            pl.BlockSpec((tm, 128), lambda i, j: (0, 0)),
            pl.BlockSpec((tm, 128), lambda i, j: (0, 0)),
        ],
        out_specs=(pl.BlockSpec((tm, tn), lambda i, j: (i, jnp.where(j == ZJ_MK, ZJ_MQ, j))),
                   pl.BlockSpec((tm, 128), lambda i, j: (i, 0)),
                   pl.BlockSpec((tn, tm), lambda i, j: (0, i))),
        scratch_shapes=[pltpu.VMEM((tm, D), BF16)],
        compiler_params=_cparams(("arbitrary", "arbitrary")),
        name="inproj",
    )(x, mod, wz, wg, conv_w, conv_b, *rope_tabs)


def _diff_attn_core(q, ks, vs, lam, gain):
    lane = lax.broadcasted_iota(jnp.int32, (1, 128), 1)
    lo = lane < DA_HD
    zero = jnp.zeros_like(q)
    q1 = jnp.where(lo, q, zero)
    q2 = jnp.where(lo, zero, q)

    tq = q.shape[0]
    s = [_dot_nt(jnp.concatenate([q1, q2], axis=0), k) for k in ks]
    mx = functools.reduce(jnp.maximum, [jnp.max(t, axis=-1, keepdims=True) for t in s])
    e = [jnp.exp(t - mx) for t in s]
    den = functools.reduce(jnp.add, [jnp.sum(t, axis=-1, keepdims=True) for t in e])
    o = functools.reduce(jnp.add, [_dot(t.astype(BF16), v) for t, v in zip(e, vs)]) * (1.0 / den)
    return _rms(o[:tq] - lam * o[tq:]) * gain


def _lambda_full(lam_ref, lam_init):
    p = lam_ref[0]
    s1 = jnp.sum(p[0:1, :] * p[1:2, :], axis=-1, keepdims=True)
    s2 = jnp.sum(p[2:3, :] * p[3:4, :], axis=-1, keepdims=True)
    return jnp.exp(s1) - jnp.exp(s2) + lam_init


def _carried(arrays):
    return [pl.BlockSpec(memory_space=pl.ANY)] * len(arrays)


def _attn_ctx_kernel(*refs, lam_init, n_carried):
    q_ref, k_ref, v_ref, lam_ref, g_ref = refs[:5]
    y_ref, ck_ref, cv_ref = refs[5 + n_carried:]
    lam = _lambda_full(lam_ref, lam_init)
    for h in range(DA_HEADS):
        hs = slice(h * 128, (h + 1) * 128)
        k = k_ref[:, hs]
        v = v_ref[:, hs]
        o = _diff_attn_core(q_ref[:, hs], [k], [v], lam, g_ref[0])
        y_ref[:, hs] = (o * (1.0 - lam_init)).astype(BF16)
        ck_ref[0, 0, h] = k.astype(F32)
        cv_ref[0, 0, h] = v.astype(F32)


def _attn_ctx(z, da_lam, norm_g, carried, *, layer, n_layers, n_seq, seq_len, lam_init):
    n_tok = z.shape[0]
    kern = functools.partial(_attn_ctx_kernel, lam_init=lam_init, n_carried=len(carried))
    blk = lambda zj: pl.BlockSpec((seq_len, COL_TILE), lambda b: (b, zj))
    cache_sds = jax.ShapeDtypeStruct((n_seq, n_layers, DA_HEADS, seq_len, 128), F32)
    cache_blk = pl.BlockSpec((1, 1, DA_HEADS, seq_len, 128), lambda b: (b, layer, 0, 0, 0))
    return pl.pallas_call(
        kern,
        out_shape=(jax.ShapeDtypeStruct((n_tok, BR_W), BF16), cache_sds, cache_sds),
        grid=(n_seq,),
        in_specs=[blk(ZJ_Q), blk(ZJ_K), blk(ZJ_V),
                  pl.BlockSpec((1, 4, DA_HD), lambda b: (layer, 0, 0)),
                  pl.BlockSpec((1, 1, 128), lambda b: (layer, 0, 0))] + _carried(carried),
        out_specs=(pl.BlockSpec((seq_len, BR_W), lambda b: (b, 0)), cache_blk, cache_blk),
        input_output_aliases={5 + n: 1 + n for n in range(len(carried))},
        compiler_params=_cparams(("arbitrary",)),
        name="attn_ctx",
    )(z, z, z, da_lam, norm_g, *carried)


def _attn_lat_kernel(q_ref, k_ref, v_ref, ck_ref, cv_ref, lam_ref, g_ref, yin_ref, y_ref, *, lam_init):
    del yin_ref
    lam = _lambda_full(lam_ref, lam_init)
    ks = [ck_ref[0, 0, 0].astype(BF16), k_ref[...]]
    vs = [cv_ref[0, 0, 0].astype(BF16), v_ref[...]]
    for t in range(q_ref.shape[0] // Q_TILE):
        rows = slice(t * Q_TILE, (t + 1) * Q_TILE)
        o = _diff_attn_core(q_ref[rows, :], ks, vs, lam, g_ref[0])
        y_ref[rows, :] = (o * (1.0 - lam_init)).astype(BF16)


def _attn_lat(z, cache_k, cache_v, da_lam, norm_g, y_da, *, layer, n_ctx_tok, n_seq, seq_len, lam_init):
    hb = COL_TILE // 128
    s0 = n_ctx_tok // seq_len
    past = cache_k.shape[3]
    kern = functools.partial(_attn_lat_kernel, lam_init=lam_init)
    zblk = lambda zj: pl.BlockSpec((seq_len, 128), lambda b, h: (s0 + b, zj * hb + h))
    return pl.pallas_call(
        kern,
        out_shape=jax.ShapeDtypeStruct(y_da.shape, y_da.dtype),
        grid=(n_seq, DA_HEADS),
        in_specs=[
            zblk(ZJ_Q), zblk(ZJ_K), zblk(ZJ_V),
            pl.BlockSpec((1, 1, 1, past, 128), lambda b, h: (b, layer, h, 0, 0)),
            pl.BlockSpec((1, 1, 1, past, 128), lambda b, h: (b, layer, h, 0, 0)),
            pl.BlockSpec((1, 4, DA_HD), lambda b, h: (layer, 0, 0)),
            pl.BlockSpec((1, 1, 128), lambda b, h: (layer, 0, 0)),
            pl.BlockSpec(memory_space=pl.ANY),
        ],
        out_specs=pl.BlockSpec((seq_len, 128), lambda b, h: (s0 + b, h)),
        input_output_aliases={7: 0},
        compiler_params=_cparams(("arbitrary", "arbitrary")),
        name="attn_lat",
    )(z, z, z, cache_k, cache_v, da_lam, norm_g, y_da)


def _split3(x):
    hi = x.astype(BF16)
    r = x - hi.astype(F32)
    mid = r.astype(BF16)
    lo = (r - mid.astype(F32)).astype(BF16)
    return hi, mid, lo


def _mlstm_kernel(*refs, n_chunks, has_init, emit_state, n_carried):
    q_ref, kt_ref, v_ref, og_ref, gt_ref, gb_ref, ng_ref = refs[:7]
    pos = 7
    if has_init:
        c0_ref, n0_ref, m0_ref = refs[pos:pos + 3]
        pos += 3
    pos += n_carried
    y_ref = refs[pos]
    pos += 1
    if emit_state:
        co_ref, no_ref, mo_ref = refs[pos:pos + 3]
        pos += 3
    h_fw, h_bw, cn_s, m_s = refs[pos:pos + 4]
    hdir = (h_fw, h_bw)

    n_streams = 2 * ML_HEADS
    for s in range(n_streams):
        d, h = divmod(s, ML_HEADS)
        if has_init:
            n_col = jnp.broadcast_to(n0_ref[0, 0, d, h:h + 1, :], (ML_HD, ML_HD)).T
            cn_s[s] = jnp.concatenate([c0_ref[0, 0, d, h], n_col], axis=1)
            m_s[s] = m0_ref[0, s:s + 1, :]
        else:
            cn_s[s] = jnp.zeros((ML_HD, 2 * ML_HD), F32)
            m_s[s] = jnp.zeros((1, 128), F32)

    row = lax.broadcasted_iota(jnp.int32, (CHUNK, CHUNK), 0)
    col = lax.broadcasted_iota(jnp.int32, (CHUNK, CHUNK), 1)
    masks = (col <= row, col >= row)
    tris = tuple(jnp.where(m, 1.0, 0.0).astype(BF16) for m in masks)
    last_row = (CHUNK - 1, 0)

    ones = jnp.ones((CHUNK, ML_HD), BF16)

    def step(c_fw):
        rows_d, prep = [], []
        for d in range(2):
            c = c_fw if d == 0 else n_chunks - 1 - c_fw
            r0 = c * CHUNK
            if not isinstance(r0, int):
                r0 = pl.multiple_of(r0, CHUNK)
            rows = pl.ds(r0, CHUNK)
            pre = gt_ref[rows, :] + gb_ref[0]
            logf = jax.nn.log_sigmoid(pre)
            hi, mid, lo = _split3(logf)
            csum = _dot(tris[d], hi) + _dot(tris[d], mid) + _dot(tris[d], lo)
            rows_d.append(rows)
            prep.append((csum, pre.T, csum.T))

        streams = [(d, h) for d in range(2) for h in range(ML_HEADS)]
        st = []
        for d, h in streams:
            s = d * ML_HEADS + h
            csum, pre_t, csum_t = prep[d]
            rows = rows_d[d]
            li, lf = d * ML_HEADS + h, 2 * ML_HEADS + d * ML_HEADS + h
            hs = slice(h * ML_HD, (h + 1) * ML_HD)
            b_t = jnp.broadcast_to(csum[:, lf:lf + 1], (CHUNK, 128))
            r_row = pre_t[li:li + 1, :] - csum_t[lf:lf + 1, :]
            m_prev = m_s[s]
            qc = q_ref[rows, hs]
            ktc = kt_ref[hs, rows]
            log_w = jnp.where(masks[d], b_t + r_row, NEG)
            inter = b_t + m_prev
            m_t = jnp.maximum(inter, jnp.broadcast_to(jnp.max(log_w, axis=-1, keepdims=True),
                                                      (CHUNK, 128)))
            st.append(dict(s=s, d=d, rows=rows, hs=hs, b_t=b_t, r_row=r_row, m_prev=m_prev, qc=qc,
                           ktc=ktc, log_w=log_w, inter=inter, m_t=m_t, qkt=_dot(qc, ktc),
                           qcn=_dot(qc, cn_s[s].astype(BF16))))

        for e in st:
            e["v_ext"] = jnp.concatenate([v_ref[e["rows"], e["hs"]], ones], axis=1)
            m_t = e["m_t"]
            qk = (e["qkt"] * jnp.exp(e["log_w"] - m_t)).astype(BF16)
            s_inter = jnp.exp(e["inter"] - m_t)
            both = _dot(qk, e["v_ext"]) + jnp.concatenate([s_inter, s_inter], axis=1) * e["qcn"]
            num, den = both[:, :ML_HD], both[:, ML_HD:]
            hdir[e["d"]][e["rows"], e["hs"]] = num / jnp.maximum(jnp.abs(den), jnp.exp(-m_t))

        for e in st:
            s, lr = e["s"], last_row[e["d"]]
            m_new = e["m_t"][lr:lr + 1, :]
            b_last = e["b_t"][lr:lr + 1, :]
            g_row = jnp.exp(b_last + e["r_row"] - m_new)
            decay = jnp.exp(b_last + e["m_prev"] - m_new)
            gkt = (e["ktc"].astype(F32) * g_row).astype(BF16)
            cn_s[s] = jnp.concatenate([decay, decay], axis=1) * cn_s[s] + _dot(gkt, e["v_ext"])
            m_s[s] = m_new

    if n_chunks <= 2:
        for c in range(n_chunks):
            step(c)
    else:
        def body(c, carry):
            step(c)
            return carry
        lax.fori_loop(0, n_chunks, body, 0)

    for c in range(n_chunks):
        rows = slice(c * CHUNK, (c + 1) * CHUNK)
        for h in range(ML_HEADS):
            hs = slice(h * ML_HD, (h + 1) * ML_HD)
            hn = _rms(h_fw[rows, hs] + h_bw[rows, hs]) * ng_ref[0]
            y_ref[rows, hs] = (og_ref[rows, hs].astype(F32) * hn.astype(F32)).astype(BF16)

    if emit_state:
        for s in range(n_streams):
            cn = cn_s[s]
            co_ref[0, 0, s] = cn[:, :ML_HD]
            no_ref[0, 0, s:s + 1, :] = cn[:, ML_HD:].T[0:1, :]
            mo_ref[0, 0, s:s + 1, :] = m_s[s]


def _mlstm(z, kt, gates, gate_b, norm_g, init, carried, *, layer, n_layers, tok0, n_seq, seq_len, emit_state):
    n_tok = z.shape[0]
    s0 = tok0 // seq_len
    n_chunks = seq_len // CHUNK
    has_init = init is not None
    n_streams = 2 * ML_HEADS
    zblk = lambda zj: pl.BlockSpec((seq_len, BR_W), lambda b: (s0 + b, zj))
    in_specs = [zblk(ZJ_MQ), pl.BlockSpec((BR_W, seq_len), lambda b: (0, s0 + b)), zblk(ZJ_MV), zblk(ZJ_MO),
                pl.BlockSpec((seq_len, 128), lambda b: (s0 + b, 0)),
                pl.BlockSpec((1, 1, 128), lambda b: (layer, 0, 0)),
                pl.BlockSpec((1, 1, 128), lambda b: (layer, 0, 0))]
    args = [z, kt, z, z, gates, gate_b, norm_g]
    if has_init:
        c0, n0, m0 = init
        in_specs += [
            pl.BlockSpec((1, 1, 2, ML_HEADS, ML_HD, ML_HD), lambda b: (b, layer, 0, 0, 0, 0)),
            pl.BlockSpec((1, 1, 2, ML_HEADS, ML_HD), lambda b: (b, layer, 0, 0, 0)),
            pl.BlockSpec((1, n_streams, 128), lambda b: (b, 0, 0)),
        ]
        args += [c0, n0, m0]
    first_carried = len(args)
    in_specs += _carried(carried)
    args += list(carried)
    out_shape = [jax.ShapeDtypeStruct((n_tok, BR_W), BF16)]
    out_specs = [pl.BlockSpec((seq_len, BR_W), lambda b: (s0 + b, 0))]
    if emit_state:
        out_shape += [jax.ShapeDtypeStruct((n_seq, n_layers, n_streams, ML_HD, ML_HD), F32),
                      jax.ShapeDtypeStruct((n_seq, n_layers, n_streams, ML_HD), F32),
                      jax.ShapeDtypeStruct((n_seq, n_layers, n_streams, 128), F32)]
        out_specs += [pl.BlockSpec((1, 1, n_streams, ML_HD, ML_HD), lambda b: (b, layer, 0, 0, 0)),
                      pl.BlockSpec((1, 1, n_streams, ML_HD), lambda b: (b, layer, 0, 0)),
                      pl.BlockSpec((1, 1, n_streams, 128), lambda b: (b, layer, 0, 0))]
        aliases = {first_carried + n: 1 + n for n in range(len(carried))}
    else:
        aliases = {first_carried: 0}
    kern = functools.partial(_mlstm_kernel, n_chunks=n_chunks, has_init=has_init, emit_state=emit_state,
                             n_carried=len(carried))
    return pl.pallas_call(
        kern,
        out_shape=tuple(out_shape),
        grid=(n_seq,),
        in_specs=in_specs,
        out_specs=tuple(out_specs),
        input_output_aliases=aliases,
        scratch_shapes=[pltpu.VMEM((seq_len, BR_W), F32), pltpu.VMEM((seq_len, BR_W), F32),
                        pltpu.VMEM((n_streams, ML_HD, 2 * ML_HD), F32),
                        pltpu.VMEM((n_streams, 1, 128), F32)],
        compiler_params=_cparams(("arbitrary",)),
        name="mlstm_init" if has_init else "mlstm_zero",
    )(*args)


def _merge_kernel(x_ref, mod_ref, yda_ref, yml_ref, sgu_ref, sgv_ref, g0_ref, g1_ref, g2_ref,
                  sgn_ref, sgw_ref, sgb_ref, wb_ref, wo_ref, o_ref):
    D = x_ref.shape[1]
    tm = x_ref.shape[0]
    v = sgv_ref[...].astype(F32)
    vc = v - jnp.mean(v, axis=-1, keepdims=True)
    sv = (vc * lax.rsqrt(jnp.mean(vc * vc, axis=-1, keepdims=True) + EPS) * sgn_ref[0]).astype(BF16)
    mixed = []
    for c in range(tm // CHUNK):
        rows = slice(c * CHUNK, (c + 1) * CHUNK)
        groups = [_dot(sgw_ref[0, g], sv[rows, g * 128:(g + 1) * 128]) for g in range(SG_GROUPS)]
        mixed.append(jnp.concatenate(groups, axis=1) + sgb_ref[0])
    y_sg = (sgu_ref[...].astype(F32) * jnp.concatenate(mixed, axis=0)).astype(BF16)

    m = g0_ref[...].astype(F32) * _dot(yda_ref[...], wb_ref[0, 0])
    m = m + g1_ref[...].astype(F32) * _dot(yml_ref[...], wb_ref[0, 1])
    m = m + g2_ref[...].astype(F32) * _dot(y_sg, wb_ref[0, 2])
    out = _dot(m.astype(BF16), wo_ref[0])
    o_ref[...] = x_ref[...] + mod_ref[0, :, 2 * D:3 * D] * out


def _mod_row_fn(layer, tile, n_ctx_tok, lat_len):
    def f(i):
        t0 = i * tile
        return layer * 8 + jnp.where(t0 < n_ctx_tok, 0, 1 + (t0 - n_ctx_tok) // lat_len)
    return f


def _merge(x, mod, y_da, y_ml, z, sg_norm_g, sg_w, sg_bias, w_branch, w_out, *, layer, n_ctx_tok, lat_len):
    n_tok, D = x.shape
    tm = MERGE_TILE
    mrow = _mod_row_fn(layer, tm, n_ctx_tok, lat_len)
    gate_blk = lambda n: pl.BlockSpec((tm, D), lambda i: (i, ZJ_GATE0 * COL_TILE // D + n))
    return pl.pallas_call(
        _merge_kernel,
        out_shape=jax.ShapeDtypeStruct((n_tok, D), F32),
        grid=(n_tok // tm,),
        in_specs=[
            pl.BlockSpec((tm, D), lambda i: (i, 0)),
            pl.BlockSpec((1, 1, 6 * D), lambda i: (mrow(i), 0, 0)),
            pl.BlockSpec((tm, BR_W), lambda i: (i, 0)),
            pl.BlockSpec((tm, BR_W), lambda i: (i, 0)),
            pl.BlockSpec((tm, COL_TILE), lambda i: (i, ZJ_SGU)),
            pl.BlockSpec((tm, COL_TILE), lambda i: (i, ZJ_SGV)),
            gate_blk(0), gate_blk(1), gate_blk(2),
            pl.BlockSpec((1, 1, BR_W), lambda i: (layer, 0, 0)),
            pl.BlockSpec((1, SG_GROUPS, CHUNK, CHUNK), lambda i: (layer, 0, 0, 0)),
            pl.BlockSpec((1, CHUNK, BR_W), lambda i: (layer, 0, 0)),
            pl.BlockSpec((1, N_BRANCH, BR_W, D), lambda i: (layer, 0, 0, 0)),
            pl.BlockSpec((1, D, D), lambda i: (layer, 0, 0)),
        ],
        out_specs=pl.BlockSpec((tm, D), lambda i: (i, 0)),
        compiler_params=_cparams(("arbitrary",)),
        name="merge",
    )(x, mod, y_da, y_ml, z, z, z, z, z, sg_norm_g, sg_w, sg_bias, w_branch, w_out)


def _ffn_kernel(x_ref, mod_ref, wa_ref, wg_ref, cwa_ref, cwg_ref, cba_ref, cbg_ref, wd_ref, fg_ref,
                *rest, n_ctx_tiles, ctx_len, n_chunks, final_norm):
    out_refs, (h_scr, acc, u0, u1) = rest[:-4], rest[-4:]
    ubufs = (u0, u1)
    i = pl.program_id(0)
    c = pl.program_id(1)
    D = x_ref.shape[1]
    is_lat = i >= n_ctx_tiles

    def run(par, up, down):
        if up:
            h = h_scr[...]
            ubufs[par][0] = _dot(h, wa_ref[0].astype(BF16))
            ubufs[par][1] = _dot(h, wg_ref[0].astype(BF16))
        if down:
            ua = _seq_conv3(ubufs[1 - par][0], cwa_ref, cba_ref, is_lat, ctx_len)
            ug = _seq_conv3(ubufs[1 - par][1], cwg_ref, cbg_ref, is_lat, ctx_len)
            a = (_silu(ua) * ug).astype(BF16)
            acc[...] += _dot(a, wd_ref[0].astype(BF16))

    @pl.when(c == 0)
    def _():
        h = _rms(x_ref[...]) * (1.0 + mod_ref[0, :, 4 * D:5 * D]) + mod_ref[0, :, 3 * D:4 * D]
        h_scr[...] = h.astype(BF16)
        acc[...] = jnp.zeros_like(acc)
        run(0, True, False)

    steady = (c >= 1) & (c < n_chunks)
    for par in range(2):
        @pl.when(steady & (c % 2 == par))
        def _():
            run(par, True, True)

    last = c == n_chunks

    @pl.when(last)
    def _():
        run(n_chunks % 2, False, True)

    if final_norm:
        yp_ref, ys_ref = out_refs

        def result():
            return _rms(x_ref[...] + mod_ref[0, :, 5 * D:6 * D] * acc[...]) * fg_ref[...]

        @pl.when(last & jnp.logical_not(is_lat))
        def _():
            yp_ref[...] = result()

        @pl.when(last & is_lat)
        def _():
            ys_ref[...] = result()
    else:
        @pl.when(last)
        def _():
            out_refs[0][...] = x_ref[...] + mod_ref[0, :, 5 * D:6 * D] * acc[...]


def _ffn(x, mod, w_up, conv_w, conv_b, w_down, final_g, *, layer, n_ctx_tok, ctx_len, lat_len, final_norm):
    n_tok, D = x.shape
    d_ff = w_down.shape[1]
    tm, tc = TOKEN_TILE, FF_TILE
    nc = d_ff // tc
    n_ctx_tiles = n_ctx_tok // tm
    mrow = _mod_row_fn(layer, tm, n_ctx_tok, lat_len)
    assert nc >= 2 and lat_len == tm
    kern = functools.partial(_ffn_kernel, n_ctx_tiles=n_ctx_tiles, ctx_len=ctx_len, n_chunks=nc,
                             final_norm=final_norm)
    up_c = lambda c: jnp.minimum(c, nc - 1)
    down_c = lambda c: jnp.maximum(c - 1, 0)
    if final_norm:
        out_shape = (jax.ShapeDtypeStruct((n_ctx_tok, D), F32), jax.ShapeDtypeStruct((n_tok - n_ctx_tok, D), F32))
        out_specs = (pl.BlockSpec((tm, D), lambda i, c: (jnp.minimum(i, n_ctx_tiles - 1), 0)),
                     pl.BlockSpec((tm, D), lambda i, c: (jnp.maximum(i - n_ctx_tiles, 0), 0)))
    else:
        out_shape = jax.ShapeDtypeStruct((n_tok, D), F32)
        out_specs = pl.BlockSpec((tm, D), lambda i, c: (i, 0))
    return pl.pallas_call(
        kern,
        out_shape=out_shape,
        grid=(n_tok // tm, nc + 1),
        in_specs=[
            pl.BlockSpec((tm, D), lambda i, c: (i, 0)),
            pl.BlockSpec((1, 1, 6 * D), lambda i, c: (mrow(i), 0, 0)),
            pl.BlockSpec((1, D, tc), lambda i, c: (layer, 0, up_c(c))),
            pl.BlockSpec((1, D, tc), lambda i, c: (layer, 0, nc + up_c(c))),
            pl.BlockSpec((1, 3, tc), lambda i, c: (layer, 0, down_c(c))),
            pl.BlockSpec((1, 3, tc), lambda i, c: (layer, 0, nc + down_c(c))),
            pl.BlockSpec((1, 1, tc), lambda i, c: (layer, 0, down_c(c))),
            pl.BlockSpec((1, 1, tc), lambda i, c: (layer, 0, nc + down_c(c))),
            pl.BlockSpec((1, tc, D), lambda i, c: (layer, down_c(c), 0)),
            pl.BlockSpec((1, D), lambda i, c: (0, 0)),
        ],
        out_specs=out_specs,
        scratch_shapes=[pltpu.VMEM((tm, D), BF16), pltpu.VMEM((tm, D), F32),
                        pltpu.VMEM((2, tm, tc), F32), pltpu.VMEM((2, tm, tc), F32)],
        compiler_params=_cparams(("arbitrary", "arbitrary")),
        name="ffn_final" if final_norm else "ffn",
    )(x, mod, w_up, w_up, conv_w, conv_w, conv_b, conv_b, w_down, final_g)


def _rope_tables(n_pos):
    t = jnp.arange(n_pos)
    nf = DA_HD // 4
    inv = ROPE_THETA ** (-jnp.arange(nf, dtype=F32) / nf)
    ang = [(t // GRID_W).astype(F32)[:, None] * inv, (t % GRID_W).astype(F32)[:, None] * inv]
    zeros = jnp.zeros((n_pos, nf), F32)

    def lanes(first, second):
        sub = jnp.concatenate([first(ang[0]), second(ang[0]), first(ang[1]), second(ang[1])], axis=1)
        return jnp.concatenate([sub, sub], axis=1)

    cos = lanes(jnp.cos, jnp.cos)
    sin_a = lanes(lambda a: -jnp.sin(a), lambda a: zeros)
    sin_b = lanes(lambda a: zeros, jnp.sin)
    return cos, sin_a, sin_b


def kernel(x_prompt, x_sample, c, cache_k, cache_v, state_C, state_n, state_m, c_ctx, w_mod, b_mod, w_in,
           da_lambda, da_norm_g, ml_conv_w, ml_conv_b, ml_gate_b, ml_norm_g, sg_norm_g, sg_w, sg_b,
           w_branch, w_out, w_up, ffn_conv_w, ffn_conv_b, w_down, final_g):
    B, S, D = x_prompt.shape
    Bd, Sd, _ = x_sample.shape
    L = w_mod.shape[0]
    n_ctx_tok = B * S
    n_streams = 2 * ML_HEADS

    x = jnp.concatenate([x_prompt.reshape(n_ctx_tok, D), x_sample.reshape(Bd * Sd, D)], axis=0)
    cond8 = jnp.concatenate([c_ctx[None, :], c, jnp.zeros((8 - 1 - Bd, D), F32)], axis=0)
    mod = _modulation(cond8, w_mod, b_mod).reshape(L * 8, 1, 6 * D)
    rope_tabs = _rope_tables(Sd)

    wz, wg = _prep_w_in(w_in)
    gate_b = jnp.pad(ml_gate_b.reshape(L, 1, -1), ((0, 0), (0, 0), (0, 128 - N_GATE_COLS)))
    sg_bias = jnp.repeat(jnp.swapaxes(sg_b, 1, 2), BR_W // SG_GROUPS, axis=2)
    sg_w16 = sg_w.astype(BF16)
    w_branch16 = w_branch.astype(BF16)
    w_out16 = w_out.astype(BF16)
    row = lambda p: p.reshape(L, 1, -1)

    caches, states = (), ()
    for l in range(L):
        lam_init = 0.8 - 0.6 * math.exp(-0.3 * l)
        z, gates, kt = _inproj(x, mod, wz, wg, ml_conv_w, row(ml_conv_b), rope_tabs, layer=l,
                               n_ctx_tok=n_ctx_tok, ctx_len=S, lat_len=Sd)

        y_da, *caches = _attn_ctx(z, da_lambda, row(da_norm_g), tuple(caches), layer=l, n_layers=L,
                                  n_seq=B, seq_len=S, lam_init=lam_init)
        y_da = _attn_lat(z, cache_k, cache_v, da_lambda, row(da_norm_g), y_da, layer=l,
                         n_ctx_tok=n_ctx_tok, n_seq=Bd, seq_len=Sd, lam_init=lam_init)

        y_ml, *states = _mlstm(z, kt, gates, gate_b, row(ml_norm_g), None, tuple(states), layer=l,
                               n_layers=L, tok0=0, n_seq=B, seq_len=S, emit_state=True)
        m0 = jnp.broadcast_to(state_m[:, l].reshape(Bd, n_streams, 1), (Bd, n_streams, 128))
        (y_ml,) = _mlstm(z, kt, gates, gate_b, row(ml_norm_g), (state_C, state_n, m0), (y_ml,), layer=l,
                         n_layers=L, tok0=n_ctx_tok, n_seq=Bd, seq_len=Sd, emit_state=False)

        x = _merge(x, mod, y_da, y_ml, z, row(sg_norm_g), sg_w16, sg_bias, w_branch16, w_out16,
                   layer=l, n_ctx_tok=n_ctx_tok, lat_len=Sd)
        x = _ffn(x, mod, w_up, ffn_conv_w, row(ffn_conv_b), w_down, final_g.reshape(1, -1), layer=l,
                 n_ctx_tok=n_ctx_tok, ctx_len=S, lat_len=Sd, final_norm=(l == L - 1))

    y_prompt, y_sample = x
    new_k, new_v = caches
    new_C, new_n, new_m = states
    return (y_prompt.reshape(B, S, D), y_sample.reshape(Bd, Sd, D), new_k, new_v,
            new_C.reshape(B, L, 2, ML_HEADS, ML_HD, ML_HD), new_n.reshape(B, L, 2, ML_HEADS, ML_HD),
            new_m[:, :, :, 0].reshape(B, L, 2, ML_HEADS))
```

```python
import functools
import math

import jax
import jax.numpy as jnp
from jax import lax
from jax.experimental import pallas as pl
from jax.experimental.pallas import tpu as pltpu

F32 = jnp.float32
BF16 = jnp.bfloat16

GRID_W = 64
DA_HEADS = 4
DA_HD = 64
ML_HEADS = 4
ML_HD = 128
CHUNK = 128
SG_GROUPS = 4
BR_W = 512
N_BRANCH = 3
ROPE_THETA = 10000.0
EPS = 1e-6
NEG = -1e30

VMEM_LIMIT_BYTES = 52 * 1024 * 1024

TOKEN_TILE = 1024
COL_TILE = 512
MERGE_TILE = 1024
FF_TILE = 256
Q_TILE = 256

ZJ_Q, ZJ_K, ZJ_V, ZJ_MQ, ZJ_MK, ZJ_MV, ZJ_MO, ZJ_SGU = range(8)
ZJ_GATE0 = 8
ZJ_SGV = 14
N_ZJ = 15


def _cparams(sem):
    return pltpu.CompilerParams(dimension_semantics=sem, vmem_limit_bytes=VMEM_LIMIT_BYTES)


def _dot(a, b):
    return jnp.dot(a, b, preferred_element_type=F32)


def _dot_nt(a, b):
    return lax.dot_general(a, b, (((1,), (1,)), ((), ())), preferred_element_type=F32)


def _dot_tn(a, b):
    return lax.dot_general(a, b, (((0,), (0,)), ((), ())), preferred_element_type=F32)


def _mod_kernel(cond_ref, w_ref, b_ref, o_ref):
    a = jax.nn.silu(cond_ref[...]).astype(BF16)
    o_ref[0] = _dot(a, w_ref[0].astype(BF16)) + b_ref[0]


def _modulation(cond8, w_mod, b_mod):
    L, D, N = w_mod.shape
    tn = 768
    return pl.pallas_call(
        _mod_kernel,
        out_shape=jax.ShapeDtypeStruct((L, 8, N), F32),
        grid=(L, N // tn),
        in_specs=[
            pl.BlockSpec((8, D), lambda l, j: (0, 0)),
            pl.BlockSpec((1, D, tn), lambda l, j: (l, 0, j)),
            pl.BlockSpec((1, 1, tn), lambda l, j: (l, 0, j)),
        ],
        out_specs=pl.BlockSpec((1, 8, tn), lambda l, j: (l, 0, j)),
        compiler_params=_cparams(("arbitrary", "arbitrary")),
        name="modulation",
    )(cond8, w_mod, b_mod.reshape(L, 1, N))


N_GATE_COLS = 4 * ML_HEADS
SRC_MLG_TILE = 7


def _wprep_kernel(a_ref, b_ref, wz_ref, wg_ref):
    t = pl.program_id(1)

    @pl.when(t < SRC_MLG_TILE)
    def _():
        wz_ref[0] = a_ref[0].T.astype(BF16)

    @pl.when(t >= SRC_MLG_TILE)
    def _():
        rows = jnp.concatenate([a_ref[0, N_GATE_COLS:, :], b_ref[0]], axis=0)
        wz_ref[0] = rows.T.astype(BF16)

    @pl.when(t == ZJ_SGU)
    def _():
        pad = jnp.zeros((128 - N_GATE_COLS, a_ref.shape[2]), F32)
        wg_ref[0] = jnp.concatenate([a_ref[0, :N_GATE_COLS, :], pad], axis=0).T.astype(BF16)


def _prep_w_in(w_in):
    L, D, n_in = w_in.shape
    assert n_in == N_ZJ * COL_TILE + N_GATE_COLS

    def src_tile(t):
        return jnp.where(t <= ZJ_SGU, t, jnp.where(t == ZJ_SGV, ZJ_SGU + 1, t + 1))

    w_t = jnp.swapaxes(w_in, 1, 2)
    tails_per_tile = COL_TILE // N_GATE_COLS
    return pl.pallas_call(
        _wprep_kernel,
        out_shape=(jax.ShapeDtypeStruct((L, D, N_ZJ * COL_TILE), BF16),
                   jax.ShapeDtypeStruct((L, D, 128), BF16)),
        grid=(L, N_ZJ),
        in_specs=[pl.BlockSpec((1, COL_TILE, D), lambda l, t: (l, src_tile(t), 0)),
                  pl.BlockSpec((1, N_GATE_COLS, D), lambda l, t: (l, (src_tile(t) + 1) * tails_per_tile, 0))],
        out_specs=(pl.BlockSpec((1, D, COL_TILE), lambda l, t: (l, 0, t)),
                   pl.BlockSpec((1, D, 128), lambda l, t: (l, 0, 0))),
        compiler_params=_cparams(("arbitrary", "arbitrary")),
        name="w_in_prep",
    )(w_t, w_t)


def _rms(x):
    return x * lax.rsqrt(jnp.mean(x * x, axis=-1, keepdims=True) + EPS)


def _sigmoid(x):
    return 0.5 * jnp.tanh(0.5 * x) + 0.5


def _silu(x):
    t = 0.5 * x
    return t * (1.0 + jnp.tanh(t))


def _seq_conv3(u, w_ref, b_ref, is_lat, ctx_len):
    rows = u.shape[0]
    prev = pltpu.roll(u, 1, axis=0)
    nxt = pltpu.roll(u, rows - 1, axis=0)
    r8 = lax.broadcasted_iota(jnp.int32, (8, 1), 0)
    interior = jnp.where(is_lat, 1.0, 0.0).astype(F32)
    n_pieces = rows // ctx_len
    pp, nn = [], []
    for p in range(n_pieces):
        lo, hi = p * ctx_len, (p + 1) * ctx_len
        keep_first = interior if p > 0 else 0.0
        keep_last = interior if p < n_pieces - 1 else 0.0
        pp += [prev[lo:lo + 8] * jnp.where(r8 == 0, keep_first, 1.0), prev[lo + 8:hi]]
        nn += [nxt[lo:hi - 8], nxt[hi - 8:hi] * jnp.where(r8 == 7, keep_last, 1.0)]
    prev = jnp.concatenate(pp, axis=0)
    nxt = jnp.concatenate(nn, axis=0)
    return prev * w_ref[0, 0:1, :] + u * w_ref[0, 1:2, :] + nxt * w_ref[0, 2:3, :] + b_ref[0]


def _inproj_kernel(x_ref, mod_ref, wz_ref, wg_ref, cw_ref, cb_ref, cos_ref, sin_ref,
                   z_ref, g_ref, kt_ref, h_scr, *, n_ctx_tiles, ctx_len):
    i = pl.program_id(0)
    j = pl.program_id(1)
    D = x_ref.shape[1]

    @pl.when(j == 0)
    def _():
        h = _rms(x_ref[...]) * (1.0 + mod_ref[0, :, D:2 * D]) + mod_ref[0, :, 0:D]
        hb = h.astype(BF16)
        h_scr[...] = hb
        g_ref[...] = _dot(hb, wg_ref[0])

    z = _dot(h_scr[...], wz_ref[0])
    is_lat = i >= n_ctx_tiles

    def rope(zz):
        parts = []
        for hd in range(zz.shape[1] // 128):
            xh = zz[:, hd * 128:(hd + 1) * 128]
            lane = lax.broadcasted_iota(jnp.int32, xh.shape, 1)
            partner = jnp.take_along_axis(xh, lane ^ (DA_HD // 4), axis=1)
            parts.append(xh * cos_ref[...] + partner * sin_ref[...])
        return jnp.concatenate(parts, axis=1)

    @pl.when((j == ZJ_Q) & is_lat)
    def _():
        z_ref[...] = (rope(z) * (DA_HD ** -0.5)).astype(BF16)

    @pl.when((j == ZJ_Q) & jnp.logical_not(is_lat))
    def _():
        z_ref[...] = (z * (DA_HD ** -0.5)).astype(BF16)

    @pl.when((j == ZJ_K) & is_lat)
    def _():
        z_ref[...] = rope(z).astype(BF16)

    @pl.when(((j == ZJ_K) & jnp.logical_not(is_lat)) | (j == ZJ_V) | (j == ZJ_MV))
    def _():
        z_ref[...] = z.astype(BF16)

    @pl.when(j == ZJ_MQ)
    def _():
        z_ref[...] = _silu(_seq_conv3(z, cw_ref, cb_ref, is_lat, ctx_len)).astype(BF16)

    @pl.when(j == ZJ_MK)
    def _():
        a = _silu(_seq_conv3(z, cw_ref, cb_ref, is_lat, ctx_len)) * (ML_HD ** -0.5)
        kt_ref[...] = a.T.astype(BF16)

    @pl.when((j == ZJ_MO) | ((j >= ZJ_GATE0) & (j < ZJ_SGV)))
    def _():
        z_ref[...] = _sigmoid(z).astype(BF16)

    @pl.when((j == ZJ_SGU) | (j == ZJ_SGV))
    def _():
        z_ref[...] = jax.nn.gelu(z, approximate=True).astype(BF16)


def _inproj(x, mod, wz, wg, conv_w, conv_b, rope_tabs, *, layer, n_ctx_tok, ctx_len, lat_len):
    n_tok, D = x.shape
    tm, tn = TOKEN_TILE, COL_TILE
    n_ctx_tiles = n_ctx_tok // tm
    assert lat_len == tm and tm % ctx_len == 0 and n_ctx_tok % tm == 0

    def mod_row(i):
        return layer * 8 + jnp.where(i < n_ctx_tiles, 0, i - n_ctx_tiles + 1)

    def conv_col(j):
        return jnp.clip(j - ZJ_MQ, 0, 1)

    kern = functools.partial(_inproj_kernel, n_ctx_tiles=n_ctx_tiles, ctx_len=ctx_len)
    return pl.pallas_call(
        kern,
        out_shape=(jax.ShapeDtypeStruct((n_tok, N_ZJ * tn), BF16),
                   jax.ShapeDtypeStruct((n_tok, 128), F32),
                   jax.ShapeDtypeStruct((tn, n_tok), BF16)),
        grid=(n_tok // tm, N_ZJ),
        in_specs=[
            pl.BlockSpec((tm, D), lambda i, j: (i, 0)),
            pl.BlockSpec((1, 1, 2 * D), lambda i, j: (mod_row(i), 0, 0)),
            pl.BlockSpec((1, D, tn), lambda i, j: (layer, 0, j)),
            pl.BlockSpec((1, D, 128), lambda i, j: (layer, 0, 0)),
            pl.BlockSpec((1, 3, tn), lambda i, j: (layer, 0, conv_col(j))),
            pl.BlockSpec((1, 1, tn), lambda i, j: (layer, 0, conv_col(j))),
            pl.BlockSpec((tm, 128), lambda i, j: (0, 0)),
            pl.BlockSpec((tm, 128), lambda i, j: (0, 0)),
        ],
        out_specs=(pl.BlockSpec((tm, tn), lambda i, j: (i, jnp.where(j == ZJ_MK, ZJ_MQ, j))),
                   pl.BlockSpec((tm, 128), lambda i, j: (i, 0)),
                   pl.BlockSpec((tn, tm), lambda i, j: (0, i))),
        scratch_shapes=[pltpu.VMEM((tm, D), BF16)],
        compiler_params=_cparams(("arbitrary", "arbitrary")),
        name="inproj",
    )(x, mod, wz, wg, conv_w, conv_b, *rope_tabs)


def _diff_attn_core(q, ks, vs, lam, gain):
    lane = lax.broadcasted_iota(jnp.int32, (1, 128), 1)
    lo = lane < DA_HD
    zero = jnp.zeros_like(q)
    q1 = jnp.where(lo, q, zero)
    q2 = jnp.where(lo, zero, q)

    tq = q.shape[0]
    s = [_dot_nt(jnp.concatenate([q1, q2], axis=0), k) for k in ks]
    mx = functools.reduce(jnp.maximum, [jnp.max(t, axis=-1, keepdims=True) for t in s])
    e = [jnp.exp(t - mx) for t in s]
    den = functools.reduce(jnp.add, [jnp.sum(t, axis=-1, keepdims=True) for t in e])
    o = functools.reduce(jnp.add, [_dot(t.astype(BF16), v) for t, v in zip(e, vs)]) * (1.0 / den)
    return _rms(o[:tq] - lam * o[tq:]) * gain


def _lambda_full(lam_ref, lam_init):
    p = lam_ref[0]
    s1 = jnp.sum(p[0:1, :] * p[1:2, :], axis=-1, keepdims=True)
    s2 = jnp.sum(p[2:3, :] * p[3:4, :], axis=-1, keepdims=True)
    return jnp.exp(s1) - jnp.exp(s2) + lam_init


def _carried(arrays):
    return [pl.BlockSpec(memory_space=pl.ANY)] * len(arrays)


def _attn_ctx_kernel(*refs, lam_init, n_carried):
    q_ref, k_ref, v_ref, lam_ref, g_ref = refs[:5]
    y_ref, ck_ref, cv_ref = refs[5 + n_carried:]
    lam = _lambda_full(lam_ref, lam_init)
    for h in range(DA_HEADS):
        hs = slice(h * 128, (h + 1) * 128)
        k = k_ref[:, hs]
        v = v_ref[:, hs]
        o = _diff_attn_core(q_ref[:, hs], [k], [v], lam, g_ref[0])
        y_ref[:, hs] = (o * (1.0 - lam_init)).astype(BF16)
        ck_ref[0, 0, h] = k.astype(F32)
        cv_ref[0, 0, h] = v.astype(F32)


def _attn_ctx(z, da_lam, norm_g, carried, *, layer, n_layers, n_seq, seq_len, lam_init):
    n_tok = z.shape[0]
    kern = functools.partial(_attn_ctx_kernel, lam_init=lam_init, n_carried=len(carried))
    blk = lambda zj: pl.BlockSpec((seq_len, COL_TILE), lambda b: (b, zj))
    cache_sds = jax.ShapeDtypeStruct((n_seq, n_layers, DA_HEADS, seq_len, 128), F32)
    cache_blk = pl.BlockSpec((1, 1, DA_HEADS, seq_len, 128), lambda b: (b, layer, 0, 0, 0))
    return pl.pallas_call(
        kern,
        out_shape=(jax.ShapeDtypeStruct((n_tok, BR_W), BF16), cache_sds, cache_sds),
        grid=(n_seq,),
        in_specs=[blk(ZJ_Q), blk(ZJ_K), blk(ZJ_V),
                  pl.BlockSpec((1, 4, DA_HD), lambda b: (layer, 0, 0)),
                  pl.BlockSpec((1, 1, 128), lambda b: (layer, 0, 0))] + _carried(carried),
        out_specs=(pl.BlockSpec((seq_len, BR_W), lambda b: (b, 0)), cache_blk, cache_blk),
        input_output_aliases={5 + n: 1 + n for n in range(len(carried))},
        compiler_params=_cparams(("arbitrary",)),
        name="attn_ctx",
    )(z, z, z, da_lam, norm_g, *carried)


def _attn_lat_kernel(q_ref, k_ref, v_ref, ck_ref, cv_ref, lam_ref, g_ref, yin_ref, y_ref, *, lam_init):
    del yin_ref
    lam = _lambda_full(lam_ref, lam_init)
    ks = [jnp.concatenate([ck_ref[0, 0, 0].astype(BF16), k_ref[...]], axis=0)]
    vs = [jnp.concatenate([cv_ref[0, 0, 0].astype(BF16), v_ref[...]], axis=0)]
    for t in range(q_ref.shape[0] // Q_TILE):
        rows = slice(t * Q_TILE, (t + 1) * Q_TILE)
        o = _diff_attn_core(q_ref[rows, :], ks, vs, lam, g_ref[0])
        y_ref[rows, :] = (o * (1.0 - lam_init)).astype(BF16)


def _attn_lat(z, cache_k, cache_v, da_lam, norm_g, y_da, *, layer, n_ctx_tok, n_seq, seq_len, lam_init):
    hb = COL_TILE // 128
    s0 = n_ctx_tok // seq_len
    past = cache_k.shape[3]
    kern = functools.partial(_attn_lat_kernel, lam_init=lam_init)
    zblk = lambda zj: pl.BlockSpec((seq_len, 128), lambda b, h: (s0 + b, zj * hb + h))
    return pl.pallas_call(
        kern,
        out_shape=jax.ShapeDtypeStruct(y_da.shape, y_da.dtype),
        grid=(n_seq, DA_HEADS),
        in_specs=[
            zblk(ZJ_Q), zblk(ZJ_K), zblk(ZJ_V),
            pl.BlockSpec((1, 1, 1, past, 128), lambda b, h: (b, layer, h, 0, 0)),
            pl.BlockSpec((1, 1, 1, past, 128), lambda b, h: (b, layer, h, 0, 0)),
            pl.BlockSpec((1, 4, DA_HD), lambda b, h: (layer, 0, 0)),
            pl.BlockSpec((1, 1, 128), lambda b, h: (layer, 0, 0)),
            pl.BlockSpec(memory_space=pl.ANY),
        ],
        out_specs=pl.BlockSpec((seq_len, 128), lambda b, h: (s0 + b, h)),
        input_output_aliases={7: 0},
        compiler_params=_cparams(("arbitrary", "arbitrary")),
        name="attn_lat",
    )(z, z, z, cache_k, cache_v, da_lam, norm_g, y_da)


def _split3(x):
    hi = x.astype(BF16)
    r = x - hi.astype(F32)
    mid = r.astype(BF16)
    lo = (r - mid.astype(F32)).astype(BF16)
    return hi, mid, lo


def _mlstm_kernel(*refs, n_chunks, has_init, emit_state, n_carried):
    q_ref, kt_ref, v_ref, og_ref, gt_ref, gb_ref, ng_ref = refs[:7]
    pos = 7
    if has_init:
        c0_ref, n0_ref, m0_ref = refs[pos:pos + 3]
        pos += 3
    pos += n_carried
    y_ref = refs[pos]
    pos += 1
    if emit_state:
        co_ref, no_ref, mo_ref = refs[pos:pos + 3]
        pos += 3
    h_fw, h_bw, cn_s, m_s = refs[pos:pos + 4]
    hdir = (h_fw, h_bw)

    n_streams = 2 * ML_HEADS
    for s in range(n_streams):
        d, h = divmod(s, ML_HEADS)
        if has_init:
            n_col = jnp.broadcast_to(n0_ref[0, 0, d, h:h + 1, :], (ML_HD, ML_HD)).T
            cn_s[s] = jnp.concatenate([c0_ref[0, 0, d, h], n_col], axis=1)
            m_s[s] = m0_ref[0, s:s + 1, :]
        else:
            cn_s[s] = jnp.zeros((ML_HD, 2 * ML_HD), F32)
            m_s[s] = jnp.zeros((1, 128), F32)

    row = lax.broadcasted_iota(jnp.int32, (CHUNK, CHUNK), 0)
    col = lax.broadcasted_iota(jnp.int32, (CHUNK, CHUNK), 1)
    masks = (col <= row, col >= row)
    tris = tuple(jnp.where(m, 1.0, 0.0).astype(BF16) for m in masks)
    last_row = (CHUNK - 1, 0)

    ones = jnp.ones((CHUNK, ML_HD), BF16)

    def step(c_fw):
        rows_d, prep = [], []
        for d in range(2):
            c = c_fw if d == 0 else n_chunks - 1 - c_fw
            r0 = c * CHUNK
            if not isinstance(r0, int):
                r0 = pl.multiple_of(r0, CHUNK)
            rows = pl.ds(r0, CHUNK)
            pre = gt_ref[rows, :] + gb_ref[0]
            logf = jax.nn.log_sigmoid(pre)
            hi, mid, lo = _split3(logf)
            parts = _dot(tris[d], jnp.concatenate([hi, mid, lo], axis=1))
            csum = parts[:, :128] + parts[:, 128:256] + parts[:, 256:]
            rows_d.append(rows)
            prep.append((csum, pre.T, csum.T))

        streams = [(d, h) for d in range(2) for h in range(ML_HEADS)]
        st = []
        for d, h in streams:
            s = d * ML_HEADS + h
            csum, pre_t, csum_t = prep[d]
            rows = rows_d[d]
            li, lf = d * ML_HEADS + h, 2 * ML_HEADS + d * ML_HEADS + h
            hs = slice(h * ML_HD, (h + 1) * ML_HD)
            b_t = jnp.broadcast_to(csum[:, lf:lf + 1], (CHUNK, 128))
            r_row = pre_t[li:li + 1, :] - csum_t[lf:lf + 1, :]
            m_prev = m_s[s]
            qc = q_ref[rows, hs]
            ktc = kt_ref[hs, rows]
            log_w = jnp.where(masks[d], b_t + r_row, NEG)
            inter = b_t + m_prev
            m_t = jnp.maximum(inter, jnp.broadcast_to(jnp.max(log_w, axis=-1, keepdims=True),
                                                      (CHUNK, 128)))
            q_all = _dot(qc, jnp.concatenate([ktc, cn_s[s].astype(BF16)], axis=1))
            st.append(dict(s=s, d=d, rows=rows, hs=hs, b_t=b_t, r_row=r_row, m_prev=m_prev,
                           ktc=ktc, log_w=log_w, inter=inter, m_t=m_t, qkt=q_all[:, :CHUNK],
                           qcn=q_all[:, CHUNK:]))

        for e in st:
            s, lr = e["s"], last_row[e["d"]]
            m_t = e["m_t"]
            v_ext = jnp.concatenate([v_ref[e["rows"], e["hs"]], ones], axis=1)
            qk = (e["qkt"] * jnp.exp(e["log_w"] - m_t)).astype(BF16)
            m_new = m_t[lr:lr + 1, :]
            b_last = e["b_t"][lr:lr + 1, :]
            g_row = jnp.exp(b_last + e["r_row"] - m_new)
            gkt = (e["ktc"].astype(F32) * g_row).astype(BF16)
            both = _dot(jnp.concatenate([qk, gkt], axis=0), v_ext)
            s_inter = jnp.exp(e["inter"] - m_t)
            nd = both[:CHUNK] + jnp.concatenate([s_inter, s_inter], axis=1) * e["qcn"]
            num, den = nd[:, :ML_HD], nd[:, ML_HD:]
            hdir[e["d"]][e["rows"], e["hs"]] = num / jnp.maximum(jnp.abs(den), jnp.exp(-m_t))
            decay = jnp.exp(b_last + e["m_prev"] - m_new)
            cn_s[s] = jnp.concatenate([decay, decay], axis=1) * cn_s[s] + both[CHUNK:]
            m_s[s] = m_new

    if n_chunks <= 2:
        for c in range(n_chunks):
            step(c)
    else:
        def body(c, carry):
            step(c)
            return carry
        lax.fori_loop(0, n_chunks, body, 0)

    for c in range(n_chunks):
        rows = slice(c * CHUNK, (c + 1) * CHUNK)
        for h in range(ML_HEADS):
            hs = slice(h * ML_HD, (h + 1) * ML_HD)
            hn = _rms(h_fw[rows, hs] + h_bw[rows, hs]) * ng_ref[0]
            y_ref[rows, hs] = (og_ref[rows, hs].astype(F32) * hn.astype(F32)).astype(BF16)

    if emit_state:
        for s in range(n_streams):
            cn = cn_s[s]
            co_ref[0, 0, s] = cn[:, :ML_HD]
            no_ref[0, 0, s:s + 1, :] = cn[:, ML_HD:].T[0:1, :]
            mo_ref[0, 0, s:s + 1, :] = m_s[s]


def _mlstm(z, kt, gates, gate_b, norm_g, init, carried, *, layer, n_layers, tok0, n_seq, seq_len, emit_state):
    n_tok = z.shape[0]
    s0 = tok0 // seq_len
    n_chunks = seq_len // CHUNK
    has_init = init is not None
    n_streams = 2 * ML_HEADS
    zblk = lambda zj: pl.BlockSpec((seq_len, BR_W), lambda b: (s0 + b, zj))
    in_specs = [zblk(ZJ_MQ), pl.BlockSpec((BR_W, seq_len), lambda b: (0, s0 + b)), zblk(ZJ_MV), zblk(ZJ_MO),
                pl.BlockSpec((seq_len, 128), lambda b: (s0 + b, 0)),
                pl.BlockSpec((1, 1, 128), lambda b: (layer, 0, 0)),
                pl.BlockSpec((1, 1, 128), lambda b: (layer, 0, 0))]
    args = [z, kt, z, z, gates, gate_b, norm_g]
    if has_init:
        c0, n0, m0 = init
        in_specs += [
            pl.BlockSpec((1, 1, 2, ML_HEADS, ML_HD, ML_HD), lambda b: (b, layer, 0, 0, 0, 0)),
            pl.BlockSpec((1, 1, 2, ML_HEADS, ML_HD), lambda b: (b, layer, 0, 0, 0)),
            pl.BlockSpec((1, n_streams, 128), lambda b: (b, 0, 0)),
        ]
        args += [c0, n0, m0]
    first_carried = len(args)
    in_specs += _carried(carried)
    args += list(carried)
    out_shape = [jax.ShapeDtypeStruct((n_tok, BR_W), BF16)]
    out_specs = [pl.BlockSpec((seq_len, BR_W), lambda b: (s0 + b, 0))]
    if emit_state:
        out_shape += [jax.ShapeDtypeStruct((n_seq, n_layers, n_streams, ML_HD, ML_HD), F32),
                      jax.ShapeDtypeStruct((n_seq, n_layers, n_streams, ML_HD), F32),
                      jax.ShapeDtypeStruct((n_seq, n_layers, n_streams, 128), F32)]
        out_specs += [pl.BlockSpec((1, 1, n_streams, ML_HD, ML_HD), lambda b: (b, layer, 0, 0, 0)),
                      pl.BlockSpec((1, 1, n_streams, ML_HD), lambda b: (b, layer, 0, 0)),
                      pl.BlockSpec((1, 1, n_streams, 128), lambda b: (b, layer, 0, 0))]
        aliases = {first_carried + n: 1 + n for n in range(len(carried))}
    else:
        aliases = {first_carried: 0}
    kern = functools.partial(_mlstm_kernel, n_chunks=n_chunks, has_init=has_init, emit_state=emit_state,
                             n_carried=len(carried))
    return pl.pallas_call(
        kern,
        out_shape=tuple(out_shape),
        grid=(n_seq,),
        in_specs=in_specs,
        out_specs=tuple(out_specs),
        input_output_aliases=aliases,
        scratch_shapes=[pltpu.VMEM((seq_len, BR_W), F32), pltpu.VMEM((seq_len, BR_W), F32),
                        pltpu.VMEM((n_streams, ML_HD, 2 * ML_HD), F32),
                        pltpu.VMEM((n_streams, 1, 128), F32)],
        compiler_params=_cparams(("arbitrary",)),
        name="mlstm_init" if has_init else "mlstm_zero",
    )(*args)


def _merge_kernel(x_ref, mod_ref, yda_ref, yml_ref, sgu_ref, sgv_ref, g0_ref, g1_ref, g2_ref,
                  sgn_ref, sgw_ref, sgb_ref, wb_ref, wo_ref, o_ref):
    D = x_ref.shape[1]
    tm = x_ref.shape[0]
    v = sgv_ref[...].astype(F32)
    vc = v - jnp.mean(v, axis=-1, keepdims=True)
    sv = (vc * lax.rsqrt(jnp.mean(vc * vc, axis=-1, keepdims=True) + EPS) * sgn_ref[0]).astype(BF16)
    mixed = []
    for c in range(tm // CHUNK):
        rows = slice(c * CHUNK, (c + 1) * CHUNK)
        groups = [_dot(sgw_ref[0, g], sv[rows, g * 128:(g + 1) * 128]) for g in range(SG_GROUPS)]
        mixed.append(jnp.concatenate(groups, axis=1) + sgb_ref[0])
    y_sg = (sgu_ref[...].astype(F32) * jnp.concatenate(mixed, axis=0)).astype(BF16)

    m = g0_ref[...].astype(F32) * _dot(yda_ref[...], wb_ref[0, 0])
    m = m + g1_ref[...].astype(F32) * _dot(yml_ref[...], wb_ref[0, 1])
    m = m + g2_ref[...].astype(F32) * _dot(y_sg, wb_ref[0, 2])
    out = _dot(m.astype(BF16), wo_ref[0])
    o_ref[...] = x_ref[...] + mod_ref[0, :, 2 * D:3 * D] * out


def _mod_row_fn(layer, tile, n_ctx_tok, lat_len):
    def f(i):
        t0 = i * tile
        return layer * 8 + jnp.where(t0 < n_ctx_tok, 0, 1 + (t0 - n_ctx_tok) // lat_len)
    return f


def _merge(x, mod, y_da, y_ml, z, sg_norm_g, sg_w, sg_bias, w_branch, w_out, *, layer, n_ctx_tok, lat_len):
    n_tok, D = x.shape
    tm = MERGE_TILE
    mrow = _mod_row_fn(layer, tm, n_ctx_tok, lat_len)
    gate_blk = lambda n: pl.BlockSpec((tm, D), lambda i: (i, ZJ_GATE0 * COL_TILE // D + n))
    return pl.pallas_call(
        _merge_kernel,
        out_shape=jax.ShapeDtypeStruct((n_tok, D), F32),
        grid=(n_tok // tm,),
        in_specs=[
            pl.BlockSpec((tm, D), lambda i: (i, 0)),
            pl.BlockSpec((1, 1, 6 * D), lambda i: (mrow(i), 0, 0)),
            pl.BlockSpec((tm, BR_W), lambda i: (i, 0)),
            pl.BlockSpec((tm, BR_W), lambda i: (i, 0)),
            pl.BlockSpec((tm, COL_TILE), lambda i: (i, ZJ_SGU)),
            pl.BlockSpec((tm, COL_TILE), lambda i: (i, ZJ_SGV)),
            gate_blk(0), gate_blk(1), gate_blk(2),
            pl.BlockSpec((1, 1, BR_W), lambda i: (layer, 0, 0)),
            pl.BlockSpec((1, SG_GROUPS, CHUNK, CHUNK), lambda i: (layer, 0, 0, 0)),
            pl.BlockSpec((1, CHUNK, BR_W), lambda i: (layer, 0, 0)),
            pl.BlockSpec((1, N_BRANCH, BR_W, D), lambda i: (layer, 0, 0, 0)),
            pl.BlockSpec((1, D, D), lambda i: (layer, 0, 0)),
        ],
        out_specs=pl.BlockSpec((tm, D), lambda i: (i, 0)),
        compiler_params=_cparams(("arbitrary",)),
        name="merge",
    )(x, mod, y_da, y_ml, z, z, z, z, z, sg_norm_g, sg_w, sg_bias, w_branch, w_out)


def _ffn_kernel(x_ref, mod_ref, wa_ref, wg_ref, cwa_ref, cwg_ref, cba_ref, cbg_ref, wd_ref, fg_ref,
                *rest, n_ctx_tiles, ctx_len, n_chunks, final_norm):
    out_refs, (h_scr, acc, u0, u1) = rest[:-4], rest[-4:]
    ubufs = (u0, u1)
    i = pl.program_id(0)
    c = pl.program_id(1)
    D = x_ref.shape[1]
    is_lat = i >= n_ctx_tiles

    def run(par, up, down):
        if up:
            h = h_scr[...]
            ubufs[par][0] = _dot(h, wa_ref[0].astype(BF16))
            ubufs[par][1] = _dot(h, wg_ref[0].astype(BF16))
        if down:
            ua = _seq_conv3(ubufs[1 - par][0], cwa_ref, cba_ref, is_lat, ctx_len)
            ug = _seq_conv3(ubufs[1 - par][1], cwg_ref, cbg_ref, is_lat, ctx_len)
            a = (_silu(ua) * ug).astype(BF16)
            acc[...] += _dot(a, wd_ref[0].astype(BF16))

    @pl.when(c == 0)
    def _():
        h = _rms(x_ref[...]) * (1.0 + mod_ref[0, :, 4 * D:5 * D]) + mod_ref[0, :, 3 * D:4 * D]
        h_scr[...] = h.astype(BF16)
        acc[...] = jnp.zeros_like(acc)
        run(0, True, False)

    steady = (c >= 1) & (c < n_chunks)
    for par in range(2):
        @pl.when(steady & (c % 2 == par))
        def _():
            run(par, True, True)

    last = c == n_chunks

    @pl.when(last)
    def _():
        run(n_chunks % 2, False, True)

    if final_norm:
        yp_ref, ys_ref = out_refs

        def result():
            return _rms(x_ref[...] + mod_ref[0, :, 5 * D:6 * D] * acc[...]) * fg_ref[...]

        @pl.when(last & jnp.logical_not(is_lat))
        def _():
            yp_ref[...] = result()

        @pl.when(last & is_lat)
        def _():
            ys_ref[...] = result()
    else:
        @pl.when(last)
        def _():
            out_refs[0][...] = x_ref[...] + mod_ref[0, :, 5 * D:6 * D] * acc[...]


def _ffn(x, mod, w_up, conv_w, conv_b, w_down, final_g, *, layer, n_ctx_tok, ctx_len, lat_len, final_norm):
    n_tok, D = x.shape
    d_ff = w_down.shape[1]
    tm, tc = TOKEN_TILE, FF_TILE
    nc = d_ff // tc
    n_ctx_tiles = n_ctx_tok // tm
    mrow = _mod_row_fn(layer, tm, n_ctx_tok, lat_len)
    assert nc >= 2 and lat_len == tm
    kern = functools.partial(_ffn_kernel, n_ctx_tiles=n_ctx_tiles, ctx_len=ctx_len, n_chunks=nc,
                             final_norm=final_norm)
    up_c = lambda c: jnp.minimum(c, nc - 1)
    down_c = lambda c: jnp.maximum(c - 1, 0)
    if final_norm:
        out_shape = (jax.ShapeDtypeStruct((n_ctx_tok, D), F32), jax.ShapeDtypeStruct((n_tok - n_ctx_tok, D), F32))
        out_specs = (pl.BlockSpec((tm, D), lambda i, c: (jnp.minimum(i, n_ctx_tiles - 1), 0)),
                     pl.BlockSpec((tm, D), lambda i, c: (jnp.maximum(i - n_ctx_tiles, 0), 0)))
    else:
        out_shape = jax.ShapeDtypeStruct((n_tok, D), F32)
        out_specs = pl.BlockSpec((tm, D), lambda i, c: (i, 0))
    return pl.pallas_call(
        kern,
        out_shape=out_shape,
        grid=(n_tok // tm, nc + 1),
        in_specs=[
            pl.BlockSpec((tm, D), lambda i, c: (i, 0)),
            pl.BlockSpec((1, 1, 6 * D), lambda i, c: (mrow(i), 0, 0)),
            pl.BlockSpec((1, D, tc), lambda i, c: (layer, 0, up_c(c))),
            pl.BlockSpec((1, D, tc), lambda i, c: (layer, 0, nc + up_c(c))),
            pl.BlockSpec((1, 3, tc), lambda i, c: (layer, 0, down_c(c))),
            pl.BlockSpec((1, 3, tc), lambda i, c: (layer, 0, nc + down_c(c))),
            pl.BlockSpec((1, 1, tc), lambda i, c: (layer, 0, down_c(c))),
            pl.BlockSpec((1, 1, tc), lambda i, c: (layer, 0, nc + down_c(c))),
            pl.BlockSpec((1, tc, D), lambda i, c: (layer, down_c(c), 0)),
            pl.BlockSpec((1, D), lambda i, c: (0, 0)),
        ],
        out_specs=out_specs,
        scratch_shapes=[pltpu.VMEM((tm, D), BF16), pltpu.VMEM((tm, D), F32),
                        pltpu.VMEM((2, tm, tc), F32), pltpu.VMEM((2, tm, tc), F32)],
        compiler_params=_cparams(("arbitrary", "arbitrary")),
        name="ffn_final" if final_norm else "ffn",
    )(x, mod, w_up, w_up, conv_w, conv_w, conv_b, conv_b, w_down, final_g)


def _rope_tables(n_pos):
    t = jnp.arange(n_pos)
    nf = DA_HD // 4
    inv = ROPE_THETA ** (-jnp.arange(nf, dtype=F32) / nf)
    ang = [(t // GRID_W).astype(F32)[:, None] * inv, (t % GRID_W).astype(F32)[:, None] * inv]

    def lanes(first, second):
        sub = jnp.concatenate([first(ang[0]), second(ang[0]), first(ang[1]), second(ang[1])], axis=1)
        return jnp.concatenate([sub, sub], axis=1)

    cos = lanes(jnp.cos, jnp.cos)
    sin = lanes(lambda a: -jnp.sin(a), jnp.sin)
    return cos, sin


def kernel(x_prompt, x_sample, c, cache_k, cache_v, state_C, state_n, state_m, c_ctx, w_mod, b_mod, w_in,
           da_lambda, da_norm_g, ml_conv_w, ml_conv_b, ml_gate_b, ml_norm_g, sg_norm_g, sg_w, sg_b,
           w_branch, w_out, w_up, ffn_conv_w, ffn_conv_b, w_down, final_g):
    B, S, D = x_prompt.shape
    Bd, Sd, _ = x_sample.shape
    L = w_mod.shape[0]
    n_ctx_tok = B * S
    n_streams = 2 * ML_HEADS

    x = jnp.concatenate([x_prompt.reshape(n_ctx_tok, D), x_sample.reshape(Bd * Sd, D)], axis=0)
    cond8 = jnp.concatenate([c_ctx[None, :], c, jnp.zeros((8 - 1 - Bd, D), F32)], axis=0)
    mod = _modulation(cond8, w_mod, b_mod).reshape(L * 8, 1, 6 * D)
    rope_tabs = _rope_tables(Sd)

    wz, wg = _prep_w_in(w_in)
    gate_b = jnp.pad(ml_gate_b.reshape(L, 1, -1), ((0, 0), (0, 0), (0, 128 - N_GATE_COLS)))
    sg_bias = jnp.repeat(jnp.swapaxes(sg_b, 1, 2), BR_W // SG_GROUPS, axis=2)
    sg_w16 = sg_w.astype(BF16)
    w_branch16 = w_branch.astype(BF16)
    w_out16 = w_out.astype(BF16)
    row = lambda p: p.reshape(L, 1, -1)

    caches, states = (), ()
    for l in range(L):
        lam_init = 0.8 - 0.6 * math.exp(-0.3 * l)
        z, gates, kt = _inproj(x, mod, wz, wg, ml_conv_w, row(ml_conv_b), rope_tabs, layer=l,
                               n_ctx_tok=n_ctx_tok, ctx_len=S, lat_len=Sd)

        y_da, *caches = _attn_ctx(z, da_lambda, row(da_norm_g), tuple(caches), layer=l, n_layers=L,
                                  n_seq=B, seq_len=S, lam_init=lam_init)
        y_da = _attn_lat(z, cache_k, cache_v, da_lambda, row(da_norm_g), y_da, layer=l,
                         n_ctx_tok=n_ctx_tok, n_seq=Bd, seq_len=Sd, lam_init=lam_init)

        y_ml, *states = _mlstm(z, kt, gates, gate_b, row(ml_norm_g), None, tuple(states), layer=l,
                               n_layers=L, tok0=0, n_seq=B, seq_len=S, emit_state=True)
        m0 = jnp.broadcast_to(state_m[:, l].reshape(Bd, n_streams, 1), (Bd, n_streams, 128))
        (y_ml,) = _mlstm(z, kt, gates, gate_b, row(ml_norm_g), (state_C, state_n, m0), (y_ml,), layer=l,
                         n_layers=L, tok0=n_ctx_tok, n_seq=Bd, seq_len=Sd, emit_state=False)

        x = _merge(x, mod, y_da, y_ml, z, row(sg_norm_g), sg_w16, sg_bias, w_branch16, w_out16,
                   layer=l, n_ctx_tok=n_ctx_tok, lat_len=Sd)
        x = _ffn(x, mod, w_up, ffn_conv_w, row(ffn_conv_b), w_down, final_g.reshape(1, -1), layer=l,
                 n_ctx_tok=n_ctx_tok, ctx_len=S, lat_len=Sd, final_norm=(l == L - 1))

    y_prompt, y_sample = x
    new_k, new_v = caches
    new_C, new_n, new_m = states
    return (y_prompt.reshape(B, S, D), y_sample.reshape(Bd, Sd, D), new_k, new_v,
            new_C.reshape(B, L, 2, ML_HEADS, ML_HD, ML_HD), new_n.reshape(B, L, 2, ML_HEADS, ML_HD),
            new_m[:, :, :, 0].reshape(B, L, 2, ML_HEADS))
```
